```python
import jax, jax.numpy as jnp
from jax import lax
import numpy as np

D_MODEL = 2048
BATCH = 8
SEQ = 4096
DEPTH = 2
DEC_BATCH = 1
DEC_SEQ = 16384
PAST_LEN = 128

HEAD_DIM = 64
ATT_HEADS = 12
ATT_WIDTH = ATT_HEADS * HEAD_DIM
LRU_WIDTH = D_MODEL - ATT_WIDTH
LRU_BLOCKS = 10
LRU_BLOCK = LRU_WIDTH // LRU_BLOCKS
IN_WIDTH = 3 * ATT_WIDTH + 2 * LRU_WIDTH
D_FF = 5632
CONV_WIDTH = 4
CONV_LEFT = 2
LRU_C = 8.0
ROPE_THETA = 500000.0
ROT_DIM = HEAD_DIM // 4
DILATED_PATTERNS = ((128, 1), (512, 4), (2048, 16))
ALPHA = (2 * DEPTH) ** 0.25
BETA = (8 * DEPTH) ** -0.25
NORM_EPS = 1e-5
NEG_INF = -1e30

kernel_name = 'hybrid_dilated_rglru_macaron_encoder'


def layer_norm(x, g, b):
    xf = x.astype(jnp.float32)
    mu = jnp.mean(xf, axis=-1, keepdims=True)
    var = jnp.mean(jnp.square(xf - mu), axis=-1, keepdims=True)
    return ((xf - mu) * lax.rsqrt(var + NORM_EPS) * g + b).astype(x.dtype)


def rms_norm(x, g):
    xf = x.astype(jnp.float32)
    ms = jnp.mean(jnp.square(xf), axis=-1, keepdims=True)
    return (xf * lax.rsqrt(ms + NORM_EPS) * g).astype(x.dtype)


def swiglu(x, w_gate, w_up, w_down):
    return (jax.nn.silu(x @ w_gate) * (x @ w_up)) @ w_down


def partial_rotary(x):
    S = x.shape[1]
    inv_freq = ROPE_THETA ** (-jnp.arange(0, ROT_DIM, 2, dtype=jnp.float32) / ROT_DIM)
    ang = jnp.arange(S, dtype=jnp.float32)[:, None] * inv_freq[None, :]
    cos = jnp.cos(ang)[None, :, None, :]
    sin = jnp.sin(ang)[None, :, None, :]
    xr = x[..., :ROT_DIM].astype(jnp.float32)
    x1, x2 = xr[..., :ROT_DIM // 2], xr[..., ROT_DIM // 2:]
    rot = jnp.concatenate([x1 * cos - x2 * sin, x2 * cos + x1 * sin], axis=-1).astype(x.dtype)
    return jnp.concatenate([rot, x[..., ROT_DIM:]], axis=-1)


def banded_window_attention(q, k, v, radius):
    n, L, h, dh = q.shape
    blk = radius
    nb = -(-L // blk)
    lp = nb * blk
    qb = jnp.pad(q, ((0, 0), (0, lp - L), (0, 0), (0, 0))).reshape(n, nb, blk, h, dh)
    pad_kv = ((0, 0), (blk, lp - L + blk), (0, 0), (0, 0))
    kb = jnp.pad(k, pad_kv).reshape(n, nb + 2, blk, h, dh)
    vb = jnp.pad(v, pad_kv).reshape(n, nb + 2, blk, h, dh)
    kw = jnp.concatenate([kb[:, :-2], kb[:, 1:-1], kb[:, 2:]], axis=2)
    vw = jnp.concatenate([vb[:, :-2], vb[:, 1:-1], vb[:, 2:]], axis=2)
    s = jnp.einsum('nbqhd,nbkhd->nbhqk', qb, kw,
                   preferred_element_type=jnp.float32) * (dh ** -0.5)
    qpos = jnp.arange(nb)[:, None, None] * blk + jnp.arange(blk)[None, :, None]
    kpos = jnp.arange(nb)[:, None, None] * blk + jnp.arange(3 * blk)[None, None, :] - blk
    mask = (jnp.abs(kpos - qpos) <= radius) & (kpos >= 0) & (kpos < L)
    s = jnp.where(mask[None, :, None], s, NEG_INF)
    lse = jax.nn.logsumexp(s, axis=-1)
    p = jnp.exp(s - lse[..., None])
    o = jnp.einsum('nbhqk,nbkhd->nbqhd', p.astype(v.dtype), vw).reshape(n, lp, h, dh)[:, :L]
    lse = lse.transpose(0, 1, 3, 2).reshape(n, lp, h)[:, :L]
    return o, lse


def dilated_window_mixture(q, k, v):
    B, S, H, Dh = q.shape
    outs, lses = [], []
    for window, dil in DILATED_PATTERNS:
        L = S // dil
        radius = window // (2 * dil)

        def split(t):
            return t.reshape(B, L, dil, H, Dh).transpose(0, 2, 1, 3, 4).reshape(B * dil, L, H, Dh)

        o, lse = banded_window_attention(split(q), split(k), split(v), radius)
        outs.append(o.reshape(B, dil, L, H, Dh).transpose(0, 2, 1, 3, 4).reshape(B, S, H, Dh))
        lses.append(lse.reshape(B, dil, L, H).transpose(0, 2, 1, 3).reshape(B, S, H))
    w = jax.nn.softmax(jnp.stack(lses, axis=-1), axis=-1).astype(q.dtype)
    return sum(outs[g] * w[..., g, None] for g in range(len(DILATED_PATTERNS)))


def centred_depthwise_conv(x, w, b):
    S = x.shape[1]
    xp = jnp.pad(x, ((0, 0), (CONV_LEFT, CONV_WIDTH - 1 - CONV_LEFT), (0, 0)))
    return sum(xp[:, j:j + S] * w[j] for j in range(CONV_WIDTH)) + b


def linear_recurrence(a, u, reverse):
    def combine(c1, c2):
        a1, b1 = c1
        a2, b2 = c2
        return a1 * a2, a2 * b1 + b2
    return lax.associative_scan(combine, (a, u), axis=1, reverse=reverse)[1]


def bidirectional_rg_lru(x, w_a, b_a, w_x, b_x, lam):
    B, S, _ = x.shape
    xb = x.reshape(B, S, LRU_BLOCKS, LRU_BLOCK)
    gate_a = jnp.einsum('bsnc,gncd->gbsnd', xb, w_a).reshape(2, B, S, LRU_WIDTH) + b_a[:, None, None]
    gate_x = jnp.einsum('bsnc,gncd->gbsnd', xb, w_x).reshape(2, B, S, LRU_WIDTH) + b_x[:, None, None]
    r = jax.nn.sigmoid(gate_a.astype(jnp.float32))
    i = jax.nn.sigmoid(gate_x.astype(jnp.float32))
    log_a = -LRU_C * jax.nn.softplus(-lam.astype(jnp.float32))[:, None, None] * r
    a = jnp.exp(log_a)
    u = jnp.sqrt(-jnp.expm1(2.0 * log_a)) * i * x.astype(jnp.float32)[None]
    h_fwd = linear_recurrence(a[0], u[0], False)
    h_bwd = linear_recurrence(a[1], u[1], True)
    return (h_fwd + h_bwd).astype(x.dtype)


def hybrid_mixer(x, w_in, conv_w, conv_b, lru_w_a, lru_b_a, lru_w_x, lru_b_x,
                 lru_lambda, att_norm_g, lru_norm_g, w_out):
    B, S, _ = x.shape
    z = x @ w_in
    q, k, v, xl, gl = jnp.split(
        z, [ATT_WIDTH, 2 * ATT_WIDTH, 3 * ATT_WIDTH, 3 * ATT_WIDTH + LRU_WIDTH], axis=-1)
    heads = lambda t: t.reshape(B, S, ATT_HEADS, HEAD_DIM)
    attn = dilated_window_mixture(partial_rotary(heads(q)), partial_rotary(heads(k)),
                                  heads(v)).reshape(B, S, ATT_WIDTH)
    xc = centred_depthwise_conv(xl, conv_w, conv_b)
    rec = bidirectional_rg_lru(xc, lru_w_a, lru_b_a, lru_w_x, lru_b_x, lru_lambda) * jax.nn.gelu(gl)
    merged = jnp.concatenate([rms_norm(attn, att_norm_g), rms_norm(rec, lru_norm_g)], axis=-1)
    return merged @ w_out


def encoder_trunk(x, ln1_g, ln1_b, ffn1_w_gate, ffn1_w_up, ffn1_w_down, ln2_g, ln2_b,
                  w_in, conv_w, conv_b, lru_w_a, lru_b_a, lru_w_x, lru_b_x, lru_lambda,
                  att_norm_g, lru_norm_g, w_out, ln3_g, ln3_b, ffn2_w_gate, ffn2_w_up,
                  ffn2_w_down):
    for l in range(DEPTH):
        x = layer_norm(ALPHA * x + 0.5 * swiglu(x, ffn1_w_gate[l], ffn1_w_up[l], ffn1_w_down[l]),
                       ln1_g[l], ln1_b[l])
        mix = hybrid_mixer(x, w_in[l], conv_w[l], conv_b[l], lru_w_a[l], lru_b_a[l],
                           lru_w_x[l], lru_b_x[l], lru_lambda[l], att_norm_g[l],
                           lru_norm_g[l], w_out[l])
        x = layer_norm(ALPHA * x + mix, ln2_g[l], ln2_b[l])
        x = layer_norm(ALPHA * x + 0.5 * swiglu(x, ffn2_w_gate[l], ffn2_w_up[l], ffn2_w_down[l]),
                       ln3_g[l], ln3_b[l])
    return x


def setup_inputs(seed: int = 0) -> dict:
    key = jax.random.key(seed)
    ks = jax.random.split(key, 28)
    f32 = jnp.float32
    nrm = lambda k, shape, scale: jax.random.normal(k, shape, f32) * scale
    gain = lambda k, n: 1.0 + nrm(k, (DEPTH, n), 0.05)
    col_scale = jnp.concatenate([jnp.ones((2 * ATT_WIDTH,), f32),
                                 jnp.full((ATT_WIDTH + LRU_WIDTH,), BETA, f32),
                                 jnp.ones((LRU_WIDTH,), f32)])
    a_pow_c = jax.random.uniform(ks[14], (DEPTH, 2, LRU_WIDTH), f32, 0.9, 0.999)
    a0 = a_pow_c ** (1.0 / LRU_C)
    return {
        'x_prompt': nrm(ks[0], (BATCH, SEQ, D_MODEL), 1.0),
        'x_sample': nrm(ks[1], (DEC_BATCH, DEC_SEQ, D_MODEL), 1.0),
        'ln1_g': gain(ks[2], D_MODEL),
        'ln1_b': nrm(ks[3], (DEPTH, D_MODEL), 0.02),
        'ffn1_w_gate': nrm(ks[4], (DEPTH, D_MODEL, D_FF), BETA * D_MODEL ** -0.5),
        'ffn1_w_up': nrm(ks[5], (DEPTH, D_MODEL, D_FF), BETA * D_MODEL ** -0.5),
        'ffn1_w_down': nrm(ks[6], (DEPTH, D_FF, D_MODEL), BETA * D_FF ** -0.5),
        'ln2_g': gain(ks[7], D_MODEL),
        'ln2_b': nrm(ks[8], (DEPTH, D_MODEL), 0.02),
        'w_in': nrm(ks[9], (DEPTH, D_MODEL, IN_WIDTH), D_MODEL ** -0.5) * col_scale,
        'conv_w': nrm(ks[10], (DEPTH, CONV_WIDTH, LRU_WIDTH), CONV_WIDTH ** -0.5),
        'conv_b': nrm(ks[11], (DEPTH, LRU_WIDTH), 0.02),
        'lru_w_a': nrm(ks[12], (DEPTH, 2, LRU_BLOCKS, LRU_BLOCK, LRU_BLOCK), LRU_BLOCK ** -0.5),
        'lru_b_a': nrm(ks[13], (DEPTH, 2, LRU_WIDTH), 0.05),
        'lru_w_x': nrm(ks[15], (DEPTH, 2, LRU_BLOCKS, LRU_BLOCK, LRU_BLOCK), LRU_BLOCK ** -0.5),
        'lru_b_x': nrm(ks[16], (DEPTH, 2, LRU_WIDTH), 0.05),
        'lru_lambda': jnp.log(a0) - jnp.log1p(-a0),
        'att_norm_g': gain(ks[17], ATT_WIDTH),
        'lru_norm_g': gain(ks[18], LRU_WIDTH),
        'w_out': nrm(ks[19], (DEPTH, D_MODEL, D_MODEL), BETA * D_MODEL ** -0.5),
        'ln3_g': gain(ks[20], D_MODEL),
        'ln3_b': nrm(ks[21], (DEPTH, D_MODEL), 0.02),
        'ffn2_w_gate': nrm(ks[22], (DEPTH, D_MODEL, D_FF), BETA * D_MODEL ** -0.5),
        'ffn2_w_up': nrm(ks[23], (DEPTH, D_MODEL, D_FF), BETA * D_MODEL ** -0.5),
        'ffn2_w_down': nrm(ks[24], (DEPTH, D_FF, D_MODEL), BETA * D_FF ** -0.5),
    }


def reference(x_prompt, x_sample, ln1_g, ln1_b, ffn1_w_gate, ffn1_w_up, ffn1_w_down,
              ln2_g, ln2_b, w_in, conv_w, conv_b, lru_w_a, lru_b_a, lru_w_x, lru_b_x,
              lru_lambda, att_norm_g, lru_norm_g, w_out, ln3_g, ln3_b, ffn2_w_gate,
              ffn2_w_up, ffn2_w_down):
    params = (ln1_g, ln1_b, ffn1_w_gate, ffn1_w_up, ffn1_w_down, ln2_g, ln2_b,
              w_in, conv_w, conv_b, lru_w_a, lru_b_a, lru_w_x, lru_b_x, lru_lambda,
              att_norm_g, lru_norm_g, w_out, ln3_g, ln3_b, ffn2_w_gate, ffn2_w_up,
              ffn2_w_down)
    y_prompt = encoder_trunk(x_prompt, *params)
    y_sample = encoder_trunk(x_sample, *params)
    return (y_prompt, y_sample)
```

```python
import functools

import jax
import jax.numpy as jnp
from jax import lax
from jax.experimental import pallas as pl
from jax.experimental.pallas import tpu as pltpu

HEAD_DIM = 64
ATT_HEADS = 12
ATT_WIDTH = ATT_HEADS * HEAD_DIM
LRU_BLOCKS = 10
LRU_BLOCK = 128
LRU_WIDTH = LRU_BLOCKS * LRU_BLOCK
CONV_WIDTH = 4
LRU_C = 8.0
ROPE_THETA = 500000.0
ROT_DIM = HEAD_DIM // 4
DILATED_PATTERNS = ((128, 1), (512, 4), (2048, 16))
RADIUS = 64
NORM_EPS = 1e-5
NEG_INF = -1e30

LANES = 128
HEAD_PAIRS = ATT_WIDTH // LANES
VMEM_LIMIT_BYTES = 56 * 1024 * 1024

F32 = jnp.float32
BF16 = jnp.bfloat16


def _params(*semantics):
    return pltpu.CompilerParams(dimension_semantics=semantics, vmem_limit_bytes=VMEM_LIMIT_BYTES)


def _layer_norm(y, g, b):
    mu = jnp.mean(y, axis=-1, keepdims=True)
    yc = y - mu
    var = jnp.mean(yc * yc, axis=-1, keepdims=True)
    return yc * lax.rsqrt(var + NORM_EPS) * g + b


def _rms_norm(y, g):
    ms = jnp.mean(y * y, axis=-1, keepdims=True)
    return y * lax.rsqrt(ms + NORM_EPS) * g


def _seq_bounds(row0, rows_p, len_p, len_s):
    in_p = row0 < rows_p
    lo_p = lax.div(row0, len_p) * len_p
    lo_s = rows_p + lax.div(jnp.maximum(row0 - rows_p, 0), len_s) * len_s
    lo = jnp.where(in_p, lo_p, lo_s)
    hi = jnp.where(in_p, lo_p + len_p, lo_s + len_s)
    return lo, hi


def _ffn_kernel(x_ref, wg_ref, wu_ref, wd_ref, g_ref, b_ref, o_ref, xb_ref, acc_ref, *, alpha, nj):
    j = pl.program_id(1)

    @pl.when(j == 0)
    def _():
        xb_ref[...] = x_ref[...].astype(BF16)
        acc_ref[...] = jnp.zeros_like(acc_ref)

    xb = xb_ref[...]
    gate = jnp.dot(xb, wg_ref[...], preferred_element_type=F32)
    up = jnp.dot(xb, wu_ref[...], preferred_element_type=F32)
    h = (gate * jax.nn.sigmoid(gate)) * up
    acc_ref[...] += jnp.dot(h.astype(BF16), wd_ref[...], preferred_element_type=F32)

    @pl.when(j == nj - 1)
    def _():
        y = alpha * x_ref[...] + 0.5 * acc_ref[...]
        o_ref[...] = _layer_norm(y, g_ref[...], b_ref[...])


def _ffn(x, wg, wu, wd, g, b, layer, alpha, tm=512, tf=512):
    t, d = x.shape
    f = wg.shape[-1]
    nj = f // tf
    assert t % tm == 0 and f % tf == 0
    return pl.pallas_call(
        functools.partial(_ffn_kernel, alpha=alpha, nj=nj),
        out_shape=jax.ShapeDtypeStruct((t, d), F32),
        grid=(t // tm, nj),
        in_specs=[
            pl.BlockSpec((tm, d), lambda i, j: (i, 0)),
            pl.BlockSpec((None, d, tf), lambda i, j: (layer, 0, j)),
            pl.BlockSpec((None, d, tf), lambda i, j: (layer, 0, j)),
            pl.BlockSpec((None, tf, d), lambda i, j: (layer, j, 0)),
            pl.BlockSpec((None, 1, d), lambda i, j: (layer, 0, 0)),
            pl.BlockSpec((None, 1, d), lambda i, j: (layer, 0, 0)),
        ],
        out_specs=pl.BlockSpec((tm, d), lambda i, j: (i, 0)),
        scratch_shapes=[pltpu.VMEM((tm, d), BF16), pltpu.VMEM((tm, d), F32)],
        compiler_params=_params("parallel", "arbitrary"),
        name="ffn",
    )(x, wg, wu, wd, g, b)


def _rotary(t, cos_t, sin_lo, sin_hi):
    half = ROT_DIM // 2
    out = []
    for p in range(HEAD_PAIRS):
        tp = t[:, p * LANES:(p + 1) * LANES]
        up = pltpu.roll(tp, LANES - half, 1)
        dn = pltpu.roll(tp, half, 1)
        out.append(tp * cos_t + up * sin_lo + dn * sin_hi)
    return jnp.concatenate(out, axis=1)


def _inproj_kernel(x_ref, wq_ref, wk_ref, wv_ref, wx_ref, wg_ref, cos_ref, slo_ref, shi_ref,
                   q_ref, k_ref, v_ref, xl_ref, gl_ref):
    xb = x_ref[...].astype(BF16)
    cos_t, sin_lo, sin_hi = cos_ref[...], slo_ref[...], shi_ref[...]
    q = jnp.dot(xb, wq_ref[...], preferred_element_type=F32)
    q_ref[...] = (_rotary(q, cos_t, sin_lo, sin_hi) * (HEAD_DIM ** -0.5)).astype(BF16)
    k = jnp.dot(xb, wk_ref[...], preferred_element_type=F32)
    k_ref[...] = _rotary(k, cos_t, sin_lo, sin_hi).astype(BF16)
    v_ref[...] = jnp.dot(xb, wv_ref[...], preferred_element_type=F32).astype(BF16)
    xl_ref[...] = jnp.dot(xb, wx_ref[...], preferred_element_type=F32)
    gl_ref[...] = jnp.dot(xb, wg_ref[...], preferred_element_type=F32)


def _rope_tables(n_pos):
    half = ROT_DIM // 2
    inv_freq = ROPE_THETA ** (-jnp.arange(0, ROT_DIM, 2, dtype=F32) / ROT_DIM)
    ang = jnp.arange(n_pos, dtype=F32)[:, None] * inv_freq[None, :]
    cos, sin = jnp.cos(ang), jnp.sin(ang)
    rest = HEAD_DIM - ROT_DIM
    one = jnp.ones((n_pos, rest), F32)
    zero = jnp.zeros((n_pos, rest), F32)
    zh = jnp.zeros((n_pos, half), F32)
    reps = LANES // HEAD_DIM
    cos_t = jnp.tile(jnp.concatenate([cos, cos, one], axis=1), (1, reps))
    sin_lo = jnp.tile(jnp.concatenate([-sin, zh, zero], axis=1), (1, reps))
    sin_hi = jnp.tile(jnp.concatenate([zh, sin, zero], axis=1), (1, reps))
    return cos_t, sin_lo, sin_hi


def _inproj(x, w_qkv, w_lru, tables, layer, geom, tm=512):
    t, d = x.shape
    rows_p, len_p, len_s = geom
    assert t % tm == 0 and len_p % tm == 0 and len_s % tm == 0
    n_p, bp, bs = rows_p // tm, len_p // tm, len_s // tm

    def pos_map(i):
        return (jnp.where(i < n_p, lax.rem(i, bp), lax.rem(jnp.maximum(i - n_p, 0), bs)), 0)

    def wspec(width, col):
        return pl.BlockSpec((None, d, width), lambda i: (layer, 0, col), pipeline_mode=pl.Buffered(1))

    tab_spec = pl.BlockSpec((tm, LANES), pos_map)
    att_spec = pl.BlockSpec((tm, ATT_WIDTH), lambda i: (i, 0))
    lru_spec = pl.BlockSpec((tm, LRU_WIDTH), lambda i: (i, 0))
    return pl.pallas_call(
        _inproj_kernel,
        out_shape=[jax.ShapeDtypeStruct((t, ATT_WIDTH), BF16)] * 3
        + [jax.ShapeDtypeStruct((t, LRU_WIDTH), F32)] * 2,
        grid=(t // tm,),
        in_specs=[
            pl.BlockSpec((tm, d), lambda i: (i, 0)),
            wspec(ATT_WIDTH, 0), wspec(ATT_WIDTH, 1), wspec(ATT_WIDTH, 2),
            wspec(LRU_WIDTH, 0), wspec(LRU_WIDTH, 1),
            tab_spec, tab_spec, tab_spec,
        ],
        out_specs=[att_spec, att_spec, att_spec, lru_spec, lru_spec],
        compiler_params=_params("parallel"),
        name="inproj",
    )(x, w_qkv, w_qkv, w_qkv, w_lru, w_lru, *tables)


def _attn_kernel(q_ref, km_ref, kl_ref, kr_ref, vm_ref, vl_ref, vr_ref, o_ref, lse_ref,
                 kc_ref, vc_ref, *, tq, geom):
    rows_p, len_p, len_s = geom
    row0 = pl.program_id(0) * tq
    lo, hi = _seq_bounds(row0, rows_p, len_p, len_s)

    kc_ref[0:RADIUS] = kl_ref[...]
    kc_ref[RADIUS:RADIUS + tq] = km_ref[...]
    kc_ref[RADIUS + tq:] = kr_ref[...]
    vc_ref[0:RADIUS] = vl_ref[...]
    vc_ref[RADIUS:RADIUS + tq] = vm_ref[...]
    vc_ref[RADIUS + tq:] = vr_ref[...]

    qb_rows = LANES
    kb_rows = qb_rows + 2 * RADIUS
    lane = lax.broadcasted_iota(jnp.int32, (qb_rows, LANES), 1)
    low_half = lane < HEAD_DIM
    trow = lax.broadcasted_iota(jnp.int32, (qb_rows, kb_rows), 0)
    ccol = lax.broadcasted_iota(jnp.int32, (qb_rows, kb_rows), 1)
    band = (ccol >= trow) & (ccol <= trow + 2 * RADIUS)

    for qb in range(tq // qb_rows):
        r0 = qb * qb_rows
        base = row0 + r0 - RADIUS
        valid = band & (ccol >= lo - base) & (ccol < hi - base)
        lse_tile = jnp.zeros((qb_rows, LANES), F32)
        for p in range(HEAD_PAIRS):
            cols = slice(p * LANES, (p + 1) * LANES)
            qp = q_ref[r0:r0 + qb_rows, cols]
            kp = kc_ref[r0:r0 + kb_rows, cols]
            vp = vc_ref[r0:r0 + kb_rows, cols]
            outs = []
            for hh in range(2):
                sel = low_half if hh == 0 else jnp.logical_not(low_half)
                qh = jnp.where(sel, qp, jnp.zeros_like(qp))
                s = lax.dot_general(qh, kp, (((1,), (1,)), ((), ())), preferred_element_type=F32)
                s = jnp.where(valid, s, NEG_INF)
                m = jnp.max(s, axis=1, keepdims=True)
                e = jnp.exp(s - m)
                l = jnp.sum(e, axis=1, keepdims=True)
                pv = jnp.dot(e.astype(BF16), vp, preferred_element_type=F32)
                outs.append(pv / l)
                lse_tile = jnp.where(lane == 2 * p + hh, m + jnp.log(l), lse_tile)
            o_ref[r0:r0 + qb_rows, cols] = jnp.where(low_half, outs[0], outs[1])
        lse_ref[r0:r0 + qb_rows, :] = lse_tile


def _attn_pattern(q, k, v, dil, geom, tq=256):
    t = q.shape[0]
    rows = t // dil
    rows_p, len_p, len_s = (g // dil for g in geom)
    assert rows % tq == 0 and len_p % tq == 0 and len_s % tq == 0 and tq % RADIUS == 0
    qv, kv, vv = (a.reshape(rows, dil * ATT_WIDTH) for a in (q, k, v))
    hb = tq // RADIUS
    n_halo = rows // RADIUS
    main = pl.BlockSpec((tq, ATT_WIDTH), lambda i, r: (i, r))
    left = pl.BlockSpec((RADIUS, ATT_WIDTH), lambda i, r: (jnp.maximum(i * hb - 1, 0), r))
    right = pl.BlockSpec((RADIUS, ATT_WIDTH), lambda i, r: (jnp.minimum((i + 1) * hb, n_halo - 1), r))
    o, lse = pl.pallas_call(
        functools.partial(_attn_kernel, tq=tq, geom=(rows_p, len_p, len_s)),
        out_shape=[jax.ShapeDtypeStruct((rows, dil * ATT_WIDTH), F32),
                   jax.ShapeDtypeStruct((rows, dil * LANES), F32)],
        grid=(rows // tq, dil),
        in_specs=[main, main, left, right, main, left, right],
        out_specs=[pl.BlockSpec((tq, ATT_WIDTH), lambda i, r: (i, r)),
                   pl.BlockSpec((tq, LANES), lambda i, r: (i, r))],
        scratch_shapes=[pltpu.VMEM((tq + 2 * RADIUS, ATT_WIDTH), BF16),
                        pltpu.VMEM((tq + 2 * RADIUS, ATT_WIDTH), BF16)],
        compiler_params=_params("parallel", "parallel"),
        name=f"attn_d{dil}",
    )(qv, kv, kv, kv, vv, vv, vv)
    return o.reshape(t, ATT_WIDTH), lse.reshape(t, LANES)


def _lru_kernel(xm_ref, xp_ref, xn_ref, cw_ref, cb_ref, w_ref, ba_ref, bx_ref, lam_ref, h_ref,
                carry_ref, *, tc, nc, reverse, geom):
    rows_p, len_p, len_s = geom
    c = pl.program_id(0)
    row0 = ((nc - 1 - c) if reverse else c) * tc
    lo, hi = _seq_bounds(row0, rows_p, len_p, len_s)
    at_start = row0 == lo
    at_end = row0 + tc == hi

    @pl.when(at_end if reverse else at_start)
    def _():
        carry_ref[...] = jnp.zeros_like(carry_ref)

    halo = xp_ref.shape[0]
    prev = jnp.where(at_start, 0.0, xp_ref[...])
    nxt = jnp.where(at_end, 0.0, xn_ref[...])
    ext = jnp.concatenate([prev, xm_ref[...], nxt], axis=0)
    n_ext = tc + 2 * halo
    cw = cw_ref[...]

    def tap(j):
        shift = (2 - j) % n_ext
        rolled = ext if shift == 0 else pltpu.roll(ext, shift, 0)
        return rolled[halo:halo + tc] * cw[j:j + 1]

    xc = tap(0) + tap(1) + tap(2) + tap(3) + cb_ref[...]

    rows = lax.broadcasted_iota(jnp.int32, (tc, LRU_BLOCK), 0)
    lam = lam_ref[...]
    neg = -lam
    softplus = jnp.maximum(neg, 0.0) + jnp.log1p(jnp.exp(-jnp.abs(neg)))
    cl = -LRU_C * softplus
    for n in range(LRU_BLOCKS):
        cols = slice(n * LRU_BLOCK, (n + 1) * LRU_BLOCK)
        xcb = xc[:, cols]
        gates = jnp.dot(xcb.astype(BF16), w_ref[n], preferred_element_type=F32)
        r = jax.nn.sigmoid(gates[:, :LRU_BLOCK] + ba_ref[:, cols])
        i = jax.nn.sigmoid(gates[:, LRU_BLOCK:] + bx_ref[:, cols])
        log_a = cl[:, cols] * r
        a = jnp.exp(log_a)
        u = jnp.sqrt(1.0 - a * a) * i * xcb
        k = 1
        while k < tc:
            if reverse:
                keep = rows < tc - k
                shift = tc - k
            else:
                keep = rows >= k
                shift = k
            a_sh = jnp.where(keep, pltpu.roll(a, shift, 0), 1.0)
            u_sh = jnp.where(keep, pltpu.roll(u, shift, 0), 0.0)
            u = a * u_sh + u
            a = a * a_sh
            k *= 2
        h = u + a * carry_ref[:, cols]
        h_ref[:, cols] = h
        carry_ref[:, cols] = h[0:1] if reverse else h[tc - 1:tc]


def _lru_direction(xl, cw, cb, w_gate, ba, bx, lam, layer, direction, geom, tc=256, halo=8):
    t = xl.shape[0]
    nc = t // tc
    rows_p, len_p, len_s = geom
    assert t % tc == 0 and len_p % tc == 0 and len_s % tc == 0 and tc % halo == 0
    reverse = direction == 1
    hb = tc // halo
    n_halo = t // halo

    def blk(c):
        return (nc - 1 - c) if reverse else c

    vec = pl.BlockSpec((None, None, 1, LRU_WIDTH), lambda c: (layer, direction, 0, 0))
    return pl.pallas_call(
        functools.partial(_lru_kernel, tc=tc, nc=nc, reverse=reverse, geom=geom),
        out_shape=jax.ShapeDtypeStruct((t, LRU_WIDTH), F32),
        grid=(nc,),
        in_specs=[
            pl.BlockSpec((tc, LRU_WIDTH), lambda c: (blk(c), 0)),
            pl.BlockSpec((halo, LRU_WIDTH), lambda c: (jnp.maximum(blk(c) * hb - 1, 0), 0)),
            pl.BlockSpec((halo, LRU_WIDTH), lambda c: (jnp.minimum((blk(c) + 1) * hb, n_halo - 1), 0)),
            pl.BlockSpec((None, CONV_WIDTH, LRU_WIDTH), lambda c: (layer, 0, 0)),
            pl.BlockSpec((None, 1, LRU_WIDTH), lambda c: (layer, 0, 0)),
            pl.BlockSpec((None, None, LRU_BLOCKS, LRU_BLOCK, 2 * LRU_BLOCK),
                         lambda c: (layer, direction, 0, 0, 0)),
            vec, vec, vec,
        ],
        out_specs=pl.BlockSpec((tc, LRU_WIDTH), lambda c: (blk(c), 0)),
        scratch_shapes=[pltpu.VMEM((1, LRU_WIDTH), F32)],
        compiler_params=_params("arbitrary"),
        name="lru_bwd" if reverse else "lru_fwd",
    )(xl, xl, xl, cw, cb, w_gate, ba, bx, lam)


def _merge_kernel(x_ref, o1_ref, o2_ref, o3_ref, l1_ref, l2_ref, l3_ref, hf_ref, hb_ref, gl_ref,
                  ag_ref, lg_ref, wo_ref, g_ref, b_ref, out_ref, m_ref, *, alpha):
    lses = [l1_ref[...], l2_ref[...], l3_ref[...]]
    mx = jnp.maximum(jnp.maximum(lses[0], lses[1]), lses[2])
    es = [jnp.exp(l - mx) for l in lses]
    den = es[0] + es[1] + es[2]
    ws = [e / den for e in es]

    tm = x_ref.shape[0]
    low_half = lax.broadcasted_iota(jnp.int32, (tm, LANES), 1) < HEAD_DIM
    o_refs = (o1_ref, o2_ref, o3_ref)
    pieces = []
    for p in range(HEAD_PAIRS):
        cols = slice(p * LANES, (p + 1) * LANES)
        acc = None
        for o_ref, w in zip(o_refs, ws):
            wsel = jnp.where(low_half, w[:, 2 * p:2 * p + 1], w[:, 2 * p + 1:2 * p + 2])
            term = o_ref[:, cols] * wsel
            acc = term if acc is None else acc + term
        pieces.append(acc)
    attn = jnp.concatenate(pieces, axis=1)
    m_ref[:, :ATT_WIDTH] = _rms_norm(attn, ag_ref[...]).astype(BF16)

    rec = (hf_ref[...] + hb_ref[...]) * jax.nn.gelu(gl_ref[...])
    m_ref[:, ATT_WIDTH:] = _rms_norm(rec, lg_ref[...]).astype(BF16)

    mix = jnp.dot(m_ref[...], wo_ref[...], preferred_element_type=F32)
    out_ref[...] = _layer_norm(alpha * x_ref[...] + mix, g_ref[...], b_ref[...])


def _merge(x, os_, lses, hf, hb, gl, ag, lg, wo, g, b, layer, alpha, tm=256):
    t, d = x.shape
    assert t % tm == 0

    def rows(width):
        return pl.BlockSpec((tm, width), lambda i: (i, 0))

    def vec(width):
        return pl.BlockSpec((None, 1, width), lambda i: (layer, 0, 0))

    return pl.pallas_call(
        functools.partial(_merge_kernel, alpha=alpha),
        out_shape=jax.ShapeDtypeStruct((t, d), F32),
        grid=(t // tm,),
        in_specs=[rows(d)] + [rows(ATT_WIDTH)] * 3 + [rows(LANES)] * 3 + [rows(LRU_WIDTH)] * 3
        + [vec(ATT_WIDTH), vec(LRU_WIDTH),
           pl.BlockSpec((None, d, d), lambda i: (layer, 0, 0), pipeline_mode=pl.Buffered(1)),
           vec(d), vec(d)],
        out_specs=rows(d),
        scratch_shapes=[pltpu.VMEM((tm, d), BF16)],
        compiler_params=_params("parallel"),
        name="merge",
    )(x, *os_, *lses, hf, hb, gl, ag, lg, wo, g, b)


def kernel(x_prompt, x_sample, ln1_g, ln1_b, ffn1_w_gate, ffn1_w_up, ffn1_w_down, ln2_g, ln2_b, w_in, conv_w, conv_b, lru_w_a, lru_b_a, lru_w_x, lru_b_x, lru_lambda, att_norm_g, lru_norm_g, w_out, ln3_g, ln3_b, ffn2_w_gate, ffn2_w_up, ffn2_w_down):
    nb, s, d = x_prompt.shape
    db, ds, _ = x_sample.shape
    depth = ln1_g.shape[0]
    alpha = (2 * depth) ** 0.25
    assert d == ATT_WIDTH + LRU_WIDTH
    geom = (nb * s, s, ds)

    x = jnp.concatenate([x_prompt.reshape(nb * s, d), x_sample.reshape(db * ds, d)], axis=0)
    tables = _rope_tables(max(s, ds))

    row3 = lambda a: a.reshape(depth, 1, a.shape[-1])
    ln1 = (row3(ln1_g), row3(ln1_b))
    ln2 = (row3(ln2_g), row3(ln2_b))
    ln3 = (row3(ln3_g), row3(ln3_b))
    ffn1 = tuple(w.astype(BF16) for w in (ffn1_w_gate, ffn1_w_up, ffn1_w_down))
    ffn2 = tuple(w.astype(BF16) for w in (ffn2_w_gate, ffn2_w_up, ffn2_w_down))
    w_in_b = w_in.astype(BF16)
    w_qkv, w_lru = w_in_b[:, :, :3 * ATT_WIDTH], w_in_b[:, :, 3 * ATT_WIDTH:]
    w_gate = jnp.concatenate([lru_w_a, lru_w_x], axis=-1).astype(BF16)
    vec4 = lambda a: a.reshape(depth, 2, 1, LRU_WIDTH)
    ba, bx, lam = vec4(lru_b_a), vec4(lru_b_x), vec4(lru_lambda)
    cb = row3(conv_b)
    ag, lg = row3(att_norm_g), row3(lru_norm_g)
    w_out_b = w_out.astype(BF16)

    for layer in range(depth):
        x = _ffn(x, *ffn1, *ln1, layer, alpha)
        q, k, v, xl, gl = _inproj(x, w_qkv, w_lru, tables, layer, geom)
        os_, lses = [], []
        for _, dil in DILATED_PATTERNS:
            o, lse = _attn_pattern(q, k, v, dil, geom)
            os_.append(o)
            lses.append(lse)
        hf = _lru_direction(xl, conv_w, cb, w_gate, ba, bx, lam, layer, 0, geom)
        hb = _lru_direction(xl, conv_w, cb, w_gate, ba, bx, lam, layer, 1, geom)
        x = _merge(x, os_, lses, hf, hb, gl, ag, lg, w_out_b, *ln2, layer, alpha)
        x = _ffn(x, *ffn2, *ln3, layer, alpha)

    y_prompt = x[:nb * s].reshape(nb, s, d)
    y_sample = x[nb * s:].reshape(db, ds, d)
    return (y_prompt, y_sample)
```

```python
import functools

import jax
import jax.numpy as jnp
from jax import lax
from jax.experimental import pallas as pl
from jax.experimental.pallas import tpu as pltpu

HEAD_DIM = 64
ATT_HEADS = 12
ATT_WIDTH = ATT_HEADS * HEAD_DIM
LRU_BLOCKS = 10
LRU_BLOCK = 128
LRU_WIDTH = LRU_BLOCKS * LRU_BLOCK
CONV_WIDTH = 4
LRU_C = 8.0
ROPE_THETA = 500000.0
ROT_DIM = HEAD_DIM // 4
DILATED_PATTERNS = ((128, 1), (512, 4), (2048, 16))
DILATIONS = tuple(d for _, d in DILATED_PATTERNS)
RADIUS = 64
assert all(w // (2 * d) == RADIUS for w, d in DILATED_PATTERNS)
NORM_EPS = 1e-5
NEG_INF = -1e30

LANES = 128
SUBLANES = 8
HEAD_PAIRS = ATT_WIDTH // LANES
VMEM_LIMIT_BYTES = 56 * 1024 * 1024
SCAN_PITCH_PAD = 4

F32 = jnp.float32
BF16 = jnp.bfloat16


def _params(*semantics):
    return pltpu.CompilerParams(dimension_semantics=semantics, vmem_limit_bytes=VMEM_LIMIT_BYTES)


def _layer_norm(y, g, b):
    mu = jnp.mean(y, axis=-1, keepdims=True)
    yc = y - mu
    var = jnp.mean(yc * yc, axis=-1, keepdims=True)
    return yc * lax.rsqrt(var + NORM_EPS) * g + b


def _rms_norm(y, g):
    ms = jnp.mean(y * y, axis=-1, keepdims=True)
    return y * lax.rsqrt(ms + NORM_EPS) * g


def _sigmoid(z):
    return 0.5 * jnp.tanh(0.5 * z) + 0.5


def _seq_bounds(row0, rows_p, len_p, len_s):
    in_p = row0 < rows_p
    lo_p = lax.div(row0, len_p) * len_p
    lo_s = rows_p + lax.div(jnp.maximum(row0 - rows_p, 0), len_s) * len_s
    lo = jnp.where(in_p, lo_p, lo_s)
    hi = jnp.where(in_p, lo_p + len_p, lo_s + len_s)
    return lo, hi


def _ffn_kernel(*refs, alpha, nj, n_p, split_in, split_out):
    n_x = 2 if split_in else 1
    n_o = 2 if split_out else 1
    x_refs = refs[:n_x]
    wg_ref, wu_ref, wd_ref, g_ref, b_ref = refs[n_x:n_x + 5]
    o_refs = refs[n_x + 5:n_x + 5 + n_o]
    xb_ref, acc_ref = refs[n_x + 5 + n_o:]
    i, j = pl.program_id(0), pl.program_id(1)
    groups = (i < n_p, i >= n_p)

    def start(x_ref):
        xb_ref[...] = x_ref[...].astype(BF16)
        acc_ref[...] = jnp.zeros_like(acc_ref)

    def finish(x_ref, o_ref):
        y = alpha * x_ref[...] + 0.5 * acc_ref[...]
        o_ref[...] = _layer_norm(y, g_ref[...], b_ref[...])

    if split_in:
        for x_ref, grp in zip(x_refs, groups):
            pl.when((j == 0) & grp)(functools.partial(start, x_ref))
    else:
        pl.when(j == 0)(functools.partial(start, x_refs[0]))

    xb = xb_ref[...]
    gate = jnp.dot(xb, wg_ref[...], preferred_element_type=F32)
    up = jnp.dot(xb, wu_ref[...], preferred_element_type=F32)
    h = (gate * jax.nn.sigmoid(gate)) * up
    acc_ref[...] += jnp.dot(h.astype(BF16), wd_ref[...], preferred_element_type=F32)

    last = j == nj - 1
    if split_in or split_out:
        for g_idx, grp in enumerate(groups):
            x_ref = x_refs[g_idx] if split_in else x_refs[0]
            o_ref = o_refs[g_idx] if split_out else o_refs[0]
            pl.when(last & grp)(functools.partial(finish, x_ref, o_ref))
    else:
        pl.when(last)(functools.partial(finish, x_refs[0], o_refs[0]))


def _ffn(xs, wg, wu, wd, g, b, layer, alpha, n_p_rows, split_out=False, tm=512, tf=512):
    split_in = len(xs) == 2
    d = xs[0].shape[1]
    t = sum(x.shape[0] for x in xs)
    f = wg.shape[-1]
    nj = f // tf
    assert t % tm == 0 and f % tf == 0 and n_p_rows % tm == 0
    n_p = n_p_rows // tm

    def grouped_specs():
        return [pl.BlockSpec((tm, d), lambda i, j: (jnp.minimum(i, n_p - 1), 0)),
                pl.BlockSpec((tm, d), lambda i, j: (jnp.maximum(i - n_p, 0), 0))]

    flat_spec = [pl.BlockSpec((tm, d), lambda i, j: (i, 0))]
    if split_out:
        out_shape = [jax.ShapeDtypeStruct((n_p_rows, d), F32), jax.ShapeDtypeStruct((t - n_p_rows, d), F32)]
    else:
        out_shape = [jax.ShapeDtypeStruct((t, d), F32)]
    return pl.pallas_call(
        functools.partial(_ffn_kernel, alpha=alpha, nj=nj, n_p=n_p, split_in=split_in, split_out=split_out),
        out_shape=out_shape,
        grid=(t // tm, nj),
        in_specs=(grouped_specs() if split_in else flat_spec) + [
            pl.BlockSpec((None, d, tf), lambda i, j: (layer, 0, j)),
            pl.BlockSpec((None, d, tf), lambda i, j: (layer, 0, j)),
            pl.BlockSpec((None, tf, d), lambda i, j: (layer, j, 0)),
            pl.BlockSpec((None, 1, d), lambda i, j: (layer, 0, 0)),
            pl.BlockSpec((None, 1, d), lambda i, j: (layer, 0, 0)),
        ],
        out_specs=grouped_specs() if split_out else flat_spec,
        scratch_shapes=[pltpu.VMEM((tm, d), BF16), pltpu.VMEM((tm, d), F32)],
        compiler_params=_params("arbitrary", "arbitrary"),
        name="ffn",
    )(*xs, wg, wu, wd, g, b)


def _rotary(t, cos_t, sin_lo, sin_hi):
    half = ROT_DIM // 2
    out = []
    for p in range(HEAD_PAIRS):
        tp = t[:, p * LANES:(p + 1) * LANES]
        up = pltpu.roll(tp, LANES - half, 1)
        dn = pltpu.roll(tp, half, 1)
        out.append(tp * cos_t + up * sin_lo + dn * sin_hi)
    return jnp.concatenate(out, axis=1)


def _emit_dilated(t, out_refs, stage_ref):
    rows = t.shape[0]
    for dil, out_ref in zip(DILATIONS, out_refs):
        if dil == 1:
            out_ref[...] = t.astype(BF16)
    for p in range(HEAD_PAIRS):
        stage_ref[p] = t[:, p * LANES:(p + 1) * LANES]
    for dil, out_ref in zip(DILATIONS, out_refs):
        if dil == 1:
            continue
        for r in range(dil):
            for p in range(HEAD_PAIRS):
                piece = stage_ref[p, pl.ds(r, rows // dil, stride=dil), :]
                c0 = r * ATT_WIDTH + p * LANES
                out_ref[:, c0:c0 + LANES] = piece.astype(BF16)


def _inproj_kernel(x_ref, wq_ref, wk_ref, wv_ref, wx_ref, wg_ref, cos_ref, slo_ref, shi_ref, *rest):
    n_pat = len(DILATIONS)
    q_refs, k_refs, v_refs = rest[:n_pat], rest[n_pat:2 * n_pat], rest[2 * n_pat:3 * n_pat]
    xl_ref, gl_ref, stage_ref = rest[3 * n_pat:]
    xb = x_ref[...].astype(BF16)
    cos_t, sin_lo, sin_hi = cos_ref[...], slo_ref[...], shi_ref[...]
    q = jnp.dot(xb, wq_ref[...], preferred_element_type=F32)
    _emit_dilated(_rotary(q, cos_t, sin_lo, sin_hi) * (HEAD_DIM ** -0.5), q_refs, stage_ref)
    k = jnp.dot(xb, wk_ref[...], preferred_element_type=F32)
    _emit_dilated(_rotary(k, cos_t, sin_lo, sin_hi), k_refs, stage_ref)
    _emit_dilated(jnp.dot(xb, wv_ref[...], preferred_element_type=F32), v_refs, stage_ref)
    xl_ref[...] = jnp.dot(xb, wx_ref[...], preferred_element_type=F32)
    gl_ref[...] = jnp.dot(xb, wg_ref[...], preferred_element_type=F32)


def _rope_tables(n_pos):
    half = ROT_DIM // 2
    inv_freq = ROPE_THETA ** (-jnp.arange(0, ROT_DIM, 2, dtype=F32) / ROT_DIM)
    ang = jnp.arange(n_pos, dtype=F32)[:, None] * inv_freq[None, :]
    cos, sin = jnp.cos(ang), jnp.sin(ang)
    rest = HEAD_DIM - ROT_DIM
    one = jnp.ones((n_pos, rest), F32)
    zero = jnp.zeros((n_pos, rest), F32)
    zh = jnp.zeros((n_pos, half), F32)
    reps = LANES // HEAD_DIM
    cos_t = jnp.tile(jnp.concatenate([cos, cos, one], axis=1), (1, reps))
    sin_lo = jnp.tile(jnp.concatenate([-sin, zh, zero], axis=1), (1, reps))
    sin_hi = jnp.tile(jnp.concatenate([zh, sin, zero], axis=1), (1, reps))
    return cos_t, sin_lo, sin_hi


def _inproj(x, w_qkv, w_lru, tables, layer, geom, tm=256):
    t, d = x.shape
    rows_p, len_p, len_s = geom
    assert t % tm == 0 and len_p % tm == 0 and len_s % tm == 0
    n_p, bp, bs = rows_p // tm, len_p // tm, len_s // tm

    def pos_map(i):
        return (jnp.where(i < n_p, lax.rem(i, bp), lax.rem(jnp.maximum(i - n_p, 0), bs)), 0)

    def wspec(width, col):
        return pl.BlockSpec((None, d, width), lambda i: (layer, 0, col), pipeline_mode=pl.Buffered(1))

    tab_spec = pl.BlockSpec((tm, LANES), pos_map)
    att_shapes = [jax.ShapeDtypeStruct((t // dil, dil * ATT_WIDTH), BF16) for dil in DILATIONS]
    att_specs = [pl.BlockSpec((tm // dil, dil * ATT_WIDTH), lambda i: (i, 0)) for dil in DILATIONS]
    lru_spec = pl.BlockSpec((tm, LRU_WIDTH), lambda i: (i, 0))
    outs = pl.pallas_call(
        _inproj_kernel,
        out_shape=att_shapes * 3 + [jax.ShapeDtypeStruct((t, LRU_WIDTH), F32)] * 2,
        grid=(t // tm,),
        in_specs=[
            pl.BlockSpec((tm, d), lambda i: (i, 0)),
            wspec(ATT_WIDTH, 0), wspec(ATT_WIDTH, 1), wspec(ATT_WIDTH, 2),
            wspec(LRU_WIDTH, 0), wspec(LRU_WIDTH, 1),
            tab_spec, tab_spec, tab_spec,
        ],
        out_specs=att_specs * 3 + [lru_spec, lru_spec],
        scratch_shapes=[pltpu.VMEM((HEAD_PAIRS, tm, LANES), F32)],
        compiler_params=_params("parallel"),
        name="inproj",
    )(x, w_qkv, w_qkv, w_qkv, w_lru, w_lru, *tables)
    n_pat = len(DILATIONS)
    return outs[:n_pat], outs[n_pat:2 * n_pat], outs[2 * n_pat:3 * n_pat], outs[-2], outs[-1]


def _attn_kernel(q_ref, km_ref, kl_ref, kr_ref, vm_ref, vl_ref, vr_ref, o_ref, lse_ref,
                 kc_ref, vc_ref, *, tq, geom):
    rows_p, len_p, len_s = geom
    row0 = pl.program_id(0) * tq
    lo, hi = _seq_bounds(row0, rows_p, len_p, len_s)

    kc_ref[0:RADIUS] = kl_ref[...]
    kc_ref[RADIUS:RADIUS + tq] = km_ref[...]
    kc_ref[RADIUS + tq:] = kr_ref[...]
    vc_ref[0:RADIUS] = vl_ref[...]
    vc_ref[RADIUS:RADIUS + tq] = vm_ref[...]
    vc_ref[RADIUS + tq:] = vr_ref[...]

    qb_rows = LANES
    kb_rows = qb_rows + 2 * RADIUS
    lane = lax.broadcasted_iota(jnp.int32, (qb_rows, LANES), 1)
    low_half = lane < HEAD_DIM
    trow = lax.broadcasted_iota(jnp.int32, (qb_rows, kb_rows), 0)
    ccol = lax.broadcasted_iota(jnp.int32, (qb_rows, kb_rows), 1)
    band = (ccol >= trow) & (ccol <= trow + 2 * RADIUS)

    for qb in range(tq // qb_rows):
        r0 = qb * qb_rows
        base = row0 + r0 - RADIUS
        valid = band & (ccol >= lo - base) & (ccol < hi - base)
        lse_tile = jnp.zeros((qb_rows, LANES), F32)
        for p in range(HEAD_PAIRS):
            cols = slice(p * LANES, (p + 1) * LANES)
            qp = q_ref[r0:r0 + qb_rows, cols]
            kp = kc_ref[r0:r0 + kb_rows, cols]
            vp = vc_ref[r0:r0 + kb_rows, cols]
            outs = []
            for hh in range(2):
                sel = low_half if hh == 0 else jnp.logical_not(low_half)
                qh = jnp.where(sel, qp, jnp.zeros_like(qp))
                s = lax.dot_general(qh, kp, (((1,), (1,)), ((), ())), preferred_element_type=F32)
                s = jnp.where(valid, s, NEG_INF)
                m = jnp.max(s, axis=1, keepdims=True)
                e = jnp.exp(s - m)
                l = jnp.sum(e, axis=1, keepdims=True)
                pv = jnp.dot(e.astype(BF16), vp, preferred_element_type=F32)
                outs.append(pv / l)
                lse_tile = jnp.where(lane == 2 * p + hh, m + jnp.log(l), lse_tile)
            o_ref[r0:r0 + qb_rows, cols] = jnp.where(low_half, outs[0], outs[1])
        lse_ref[r0:r0 + qb_rows, :] = lse_tile


def _attn_pattern(q, k, v, dil, geom, tq=256):
    rows = q.shape[0]
    rows_p, len_p, len_s = (g // dil for g in geom)
    assert rows % tq == 0 and len_p % tq == 0 and len_s % tq == 0 and tq % RADIUS == 0
    hb = tq // RADIUS
    n_halo = rows // RADIUS
    main = pl.BlockSpec((tq, ATT_WIDTH), lambda i, r: (i, r))
    left = pl.BlockSpec((RADIUS, ATT_WIDTH), lambda i, r: (jnp.maximum(i * hb - 1, 0), r))
    right = pl.BlockSpec((RADIUS, ATT_WIDTH), lambda i, r: (jnp.minimum((i + 1) * hb, n_halo - 1), r))
    return pl.pallas_call(
        functools.partial(_attn_kernel, tq=tq, geom=(rows_p, len_p, len_s)),
        out_shape=[jax.ShapeDtypeStruct((rows, dil * ATT_WIDTH), F32),
                   jax.ShapeDtypeStruct((rows, dil * LANES), F32)],
        grid=(rows // tq, dil),
        in_specs=[main, main, left, right, main, left, right],
        out_specs=[pl.BlockSpec((tq, ATT_WIDTH), lambda i, r: (i, r)),
                   pl.BlockSpec((tq, LANES), lambda i, r: (i, r))],
        scratch_shapes=[pltpu.VMEM((tq + 2 * RADIUS, ATT_WIDTH), BF16),
                        pltpu.VMEM((tq + 2 * RADIUS, ATT_WIDTH), BF16)],
        compiler_params=_params("parallel", "parallel"),
        name=f"attn_d{dil}",
    )(q, k, k, k, v, v, v)


def _sublane_scan(h_end, p_end, carry, sub, reverse):
    n = SUBLANES
    k = 1
    while k < n:
        if reverse:
            keep, shift = sub < n - k, n - k
        else:
            keep, shift = sub >= k, k
        h_sh = jnp.where(keep, pltpu.roll(h_end, shift, 0), 0.0)
        p_sh = jnp.where(keep, pltpu.roll(p_end, shift, 0), 1.0)
        h_end = p_end * h_sh + h_end
        p_end = p_end * p_sh
        k *= 2
    leave = h_end + p_end * carry
    if reverse:
        enter = jnp.where(sub < n - 1, pltpu.roll(leave, n - 1, 0), carry)
        return enter, leave[0:1]
    enter = jnp.where(sub >= 1, pltpu.roll(leave, 1, 0), carry)
    return enter, leave[n - 1:n]


def _lru_kernel(xm_ref, xp_ref, xn_ref, cw_ref, cb_ref, w_ref, ba_ref, bx_ref, lam_ref, h_ref,
                carry_ref, xs_ref, hs_ref, *, tc, nc, reverse, geom):
    rows_p, len_p, len_s = geom
    c = pl.program_id(0)
    row0 = ((nc - 1 - c) if reverse else c) * tc
    lo, hi = _seq_bounds(row0, rows_p, len_p, len_s)
    at_start = row0 == lo
    at_end = row0 + tc == hi

    @pl.when(at_end if reverse else at_start)
    def _():
        carry_ref[...] = jnp.zeros_like(carry_ref)

    halo = xp_ref.shape[0]
    prev = jnp.where(at_start, 0.0, xp_ref[...])
    nxt = jnp.where(at_end, 0.0, xn_ref[...])
    ext = jnp.concatenate([prev, xm_ref[...], nxt], axis=0)
    n_ext = tc + 2 * halo
    cw = cw_ref[...]

    def tap(j):
        shift = (2 - j) % n_ext
        rolled = ext if shift == 0 else pltpu.roll(ext, shift, 0)
        return rolled[halo:halo + tc] * cw[j:j + 1]

    xc = tap(0) + tap(1) + tap(2) + tap(3) + cb_ref[...]

    steps = tc // SUBLANES
    pitch = steps + SCAN_PITCH_PAD
    for j in range(SUBLANES):
        for n in range(LRU_BLOCKS):
            xs_ref[n, pitch * j:pitch * j + steps, :] = (
                xc[steps * j:steps * (j + 1), n * LRU_BLOCK:(n + 1) * LRU_BLOCK])

    sub = lax.broadcasted_iota(jnp.int32, (SUBLANES, LRU_BLOCK), 0)
    neg = -lam_ref[...]
    softplus = jnp.maximum(neg, 0.0) + jnp.log1p(jnp.exp(-jnp.abs(neg)))
    cl = -LRU_C * softplus
    order = range(steps - 1, -1, -1) if reverse else range(steps)
    for n in range(LRU_BLOCKS):
        cols = slice(n * LRU_BLOCK, (n + 1) * LRU_BLOCK)
        xq = jnp.concatenate([xs_ref[n, pl.ds(s, SUBLANES, stride=pitch), :] for s in range(steps)], axis=0)
        gates = jnp.dot(xq.astype(BF16), w_ref[n], preferred_element_type=F32)
        r = _sigmoid(gates[:, :LRU_BLOCK] + ba_ref[:, cols])
        i = _sigmoid(gates[:, LRU_BLOCK:] + bx_ref[:, cols])
        a = jnp.exp(cl[:, cols] * r)
        u = jnp.sqrt(1.0 - a * a) * i * xq

        h = jnp.zeros((SUBLANES, LRU_BLOCK), F32)
        pr = jnp.ones((SUBLANES, LRU_BLOCK), F32)
        h_loc, p_loc = {}, {}
        for s in order:
            a_s = a[SUBLANES * s:SUBLANES * (s + 1)]
            h = a_s * h + u[SUBLANES * s:SUBLANES * (s + 1)]
            pr = a_s * pr
            h_loc[s], p_loc[s] = h, pr
        enter, leave = _sublane_scan(h, pr, carry_ref[:, cols], sub, reverse)
        carry_ref[:, cols] = leave
        for s in range(steps):
            hs_ref[n, pl.ds(s, SUBLANES, stride=pitch), :] = h_loc[s] + p_loc[s] * enter

    for j in range(SUBLANES):
        for n in range(LRU_BLOCKS):
            h_ref[steps * j:steps * (j + 1), n * LRU_BLOCK:(n + 1) * LRU_BLOCK] = (
                hs_ref[n, pitch * j:pitch * j + steps, :])


def _lru_direction(xl, cw, cb, w_gate, ba, bx, lam, layer, direction, geom, tc=256, halo=8):
    t = xl.shape[0]
    nc = t // tc
    rows_p, len_p, len_s = geom
    assert t % tc == 0 and len_p % tc == 0 and len_s % tc == 0 and tc % halo == 0
    reverse = direction == 1
    hb = tc // halo
    n_halo = t // halo
    stage_rows = SUBLANES * (tc // SUBLANES + SCAN_PITCH_PAD)

    def blk(c):
        return (nc - 1 - c) if reverse else c

    vec = pl.BlockSpec((None, None, 1, LRU_WIDTH), lambda c: (layer, direction, 0, 0))
    return pl.pallas_call(
        functools.partial(_lru_kernel, tc=tc, nc=nc, reverse=reverse, geom=geom),
        out_shape=jax.ShapeDtypeStruct((t, LRU_WIDTH), F32),
        grid=(nc,),
        in_specs=[
            pl.BlockSpec((tc, LRU_WIDTH), lambda c: (blk(c), 0)),
            pl.BlockSpec((halo, LRU_WIDTH), lambda c: (jnp.maximum(blk(c) * hb - 1, 0), 0)),
            pl.BlockSpec((halo, LRU_WIDTH), lambda c: (jnp.minimum((blk(c) + 1) * hb, n_halo - 1), 0)),
            pl.BlockSpec((None, CONV_WIDTH, LRU_WIDTH), lambda c: (layer, 0, 0)),
            pl.BlockSpec((None, 1, LRU_WIDTH), lambda c: (layer, 0, 0)),
            pl.BlockSpec((None, None, LRU_BLOCKS, LRU_BLOCK, 2 * LRU_BLOCK),
                         lambda c: (layer, direction, 0, 0, 0)),
            vec, vec, vec,
        ],
        out_specs=pl.BlockSpec((tc, LRU_WIDTH), lambda c: (blk(c), 0)),
        scratch_shapes=[pltpu.VMEM((1, LRU_WIDTH), F32),
                        pltpu.VMEM((LRU_BLOCKS, stage_rows, LRU_BLOCK), F32),
                        pltpu.VMEM((LRU_BLOCKS, stage_rows, LRU_BLOCK), F32)],
        compiler_params=_params("arbitrary"),
        name="lru_bwd" if reverse else "lru_fwd",
    )(xl, xl, xl, cw, cb, w_gate, ba, bx, lam)


def _merge_kernel(*refs, alpha):
    n_pat = len(DILATIONS)
    x_ref = refs[0]
    o_refs = refs[1:1 + n_pat]
    l_refs = refs[1 + n_pat:1 + 2 * n_pat]
    hf_ref, hb_ref, gl_ref, ag_ref, lg_ref, wo_ref, g_ref, b_ref, out_ref, m_ref, os_ref, ls_ref = refs[1 + 2 * n_pat:]
    tm = x_ref.shape[0]

    for g, dil in enumerate(DILATIONS):
        if dil == 1:
            continue
        for r in range(dil):
            ls_ref[g, pl.ds(r, tm // dil, stride=dil), :] = l_refs[g][:, r * LANES:(r + 1) * LANES]
            for p in range(HEAD_PAIRS):
                c0 = r * ATT_WIDTH + p * LANES
                os_ref[g, p, pl.ds(r, tm // dil, stride=dil), :] = o_refs[g][:, c0:c0 + LANES]

    def lse_of(g):
        return l_refs[g][...] if DILATIONS[g] == 1 else ls_ref[g]

    def out_of(g, p):
        if DILATIONS[g] == 1:
            return o_refs[g][:, p * LANES:(p + 1) * LANES]
        return os_ref[g, p]

    lses = [lse_of(g) for g in range(n_pat)]
    mx = functools.reduce(jnp.maximum, lses)
    es = [jnp.exp(l - mx) for l in lses]
    den = functools.reduce(lambda a, b: a + b, es)
    ws = [e / den for e in es]

    low_half = lax.broadcasted_iota(jnp.int32, (tm, LANES), 1) < HEAD_DIM
    pieces = []
    for p in range(HEAD_PAIRS):
        acc = None
        for g, w in enumerate(ws):
            wsel = jnp.where(low_half, w[:, 2 * p:2 * p + 1], w[:, 2 * p + 1:2 * p + 2])
            term = out_of(g, p) * wsel
            acc = term if acc is None else acc + term
        pieces.append(acc)
    attn = jnp.concatenate(pieces, axis=1)
    m_ref[:, :ATT_WIDTH] = _rms_norm(attn, ag_ref[...]).astype(BF16)

    rec = (hf_ref[...] + hb_ref[...]) * jax.nn.gelu(gl_ref[...])
    m_ref[:, ATT_WIDTH:] = _rms_norm(rec, lg_ref[...]).astype(BF16)

    mix = jnp.dot(m_ref[...], wo_ref[...], preferred_element_type=F32)
    out_ref[...] = _layer_norm(alpha * x_ref[...] + mix, g_ref[...], b_ref[...])


def _merge(x, os_, lses, hf, hb, gl, ag, lg, wo, g, b, layer, alpha, tm=256):
    t, d = x.shape
    assert t % tm == 0
    n_pat = len(DILATIONS)

    def rows(width, dil=1):
        return pl.BlockSpec((tm // dil, dil * width), lambda i: (i, 0))

    def vec(width):
        return pl.BlockSpec((None, 1, width), lambda i: (layer, 0, 0))

    return pl.pallas_call(
        functools.partial(_merge_kernel, alpha=alpha),
        out_shape=jax.ShapeDtypeStruct((t, d), F32),
        grid=(t // tm,),
        in_specs=[rows(d)] + [rows(ATT_WIDTH, dil) for dil in DILATIONS]
        + [rows(LANES, dil) for dil in DILATIONS] + [rows(LRU_WIDTH)] * 3
        + [vec(ATT_WIDTH), vec(LRU_WIDTH),
           pl.BlockSpec((None, d, d), lambda i: (layer, 0, 0), pipeline_mode=pl.Buffered(1)),
           vec(d), vec(d)],
        out_specs=rows(d),
        scratch_shapes=[pltpu.VMEM((tm, d), BF16),
                        pltpu.VMEM((n_pat, HEAD_PAIRS, tm, LANES), F32),
                        pltpu.VMEM((n_pat, tm, LANES), F32)],
        compiler_params=_params("parallel"),
        name="merge",
    )(x, *os_, *lses, hf, hb, gl, ag, lg, wo, g, b)


def kernel(x_prompt, x_sample, ln1_g, ln1_b, ffn1_w_gate, ffn1_w_up, ffn1_w_down, ln2_g, ln2_b, w_in, conv_w, conv_b, lru_w_a, lru_b_a, lru_w_x, lru_b_x, lru_lambda, att_norm_g, lru_norm_g, w_out, ln3_g, ln3_b, ffn2_w_gate, ffn2_w_up, ffn2_w_down):
    nb, s, d = x_prompt.shape
    db, ds, _ = x_sample.shape
    depth = ln1_g.shape[0]
    alpha = (2 * depth) ** 0.25
    assert d == ATT_WIDTH + LRU_WIDTH
    geom = (nb * s, s, ds)

    tables = _rope_tables(max(s, ds))
    row3 = lambda a: a.reshape(depth, 1, a.shape[-1])
    ln1 = (row3(ln1_g), row3(ln1_b))
    ln2 = (row3(ln2_g), row3(ln2_b))
    ln3 = (row3(ln3_g), row3(ln3_b))
    ffn1 = tuple(w.astype(BF16) for w in (ffn1_w_gate, ffn1_w_up, ffn1_w_down))
    ffn2 = tuple(w.astype(BF16) for w in (ffn2_w_gate, ffn2_w_up, ffn2_w_down))
    w_in_b = w_in.astype(BF16)
    w_qkv, w_lru = w_in_b[:, :, :3 * ATT_WIDTH], w_in_b[:, :, 3 * ATT_WIDTH:]
    w_gate = jnp.concatenate([lru_w_a, lru_w_x], axis=-1).astype(BF16)
    vec4 = lambda a: a.reshape(depth, 2, 1, LRU_WIDTH)
    ba, bx, lam = vec4(lru_b_a), vec4(lru_b_x), vec4(lru_lambda)
    cb = row3(conv_b)
    ag, lg = row3(att_norm_g), row3(lru_norm_g)
    w_out_b = w_out.astype(BF16)

    xs = (x_prompt.reshape(nb * s, d), x_sample.reshape(db * ds, d))
    for layer in range(depth):
        (x,) = _ffn(xs, *ffn1, *ln1, layer, alpha, nb * s)
        qs, ks, vs, xl, gl = _inproj(x, w_qkv, w_lru, tables, layer, geom)
        os_, lses = [], []
        for g, dil in enumerate(DILATIONS):
            o, lse = _attn_pattern(qs[g], ks[g], vs[g], dil, geom)
            os_.append(o)
            lses.append(lse)
        hf = _lru_direction(xl, conv_w, cb, w_gate, ba, bx, lam, layer, 0, geom)
        hb = _lru_direction(xl, conv_w, cb, w_gate, ba, bx, lam, layer, 1, geom)
        x = _merge(x, os_, lses, hf, hb, gl, ag, lg, w_out_b, *ln2, layer, alpha)
        xs = _ffn((x,), *ffn2, *ln3, layer, alpha, nb * s, split_out=layer == depth - 1)

    y_prompt, y_sample = xs
    return (y_prompt.reshape(nb, s, d), y_sample.reshape(db, ds, d))
```

```python
import functools

import jax
import jax.numpy as jnp
from jax import lax
from jax.experimental import pallas as pl
from jax.experimental.pallas import tpu as pltpu

HEAD_DIM = 64
ATT_HEADS = 12
ATT_WIDTH = ATT_HEADS * HEAD_DIM
LRU_BLOCKS = 10
LRU_BLOCK = 128
LRU_WIDTH = LRU_BLOCKS * LRU_BLOCK
CONV_WIDTH = 4
CONV_LEFT = 2
LRU_C = 8.0
ROPE_THETA = 500000.0
ROT_DIM = HEAD_DIM // 4
DILATED_PATTERNS = ((128, 1), (512, 4), (2048, 16))
DILATIONS = tuple(d for _, d in DILATED_PATTERNS)
RADIUS = 64
assert all(w // (2 * d) == RADIUS for w, d in DILATED_PATTERNS)
NORM_EPS = 1e-5
NEG_INF = -1e30

LANES = 128
SUBLANES = 8
HEAD_PAIRS = ATT_WIDTH // LANES
VMEM_LIMIT_BYTES = 56 * 1024 * 1024
SCAN_PITCH_PAD = 4

F32 = jnp.float32
BF16 = jnp.bfloat16


def _params(*semantics):
    return pltpu.CompilerParams(dimension_semantics=semantics, vmem_limit_bytes=VMEM_LIMIT_BYTES)


def _layer_norm(y, g, b):
    mu = jnp.mean(y, axis=-1, keepdims=True)
    yc = y - mu
    var = jnp.mean(yc * yc, axis=-1, keepdims=True)
    return yc * lax.rsqrt(var + NORM_EPS) * g + b


def _rms_norm(y, g):
    ms = jnp.mean(y * y, axis=-1, keepdims=True)
    return y * lax.rsqrt(ms + NORM_EPS) * g


def _sigmoid(z):
    return 0.5 * jnp.tanh(0.5 * z) + 0.5


def _seq_bounds(row0, rows_p, len_p, len_s):
    in_p = row0 < rows_p
    lo_p = lax.div(row0, len_p) * len_p
    lo_s = rows_p + lax.div(jnp.maximum(row0 - rows_p, 0), len_s) * len_s
    lo = jnp.where(in_p, lo_p, lo_s)
    hi = jnp.where(in_p, lo_p + len_p, lo_s + len_s)
    return lo, hi


def _ffn_kernel(*refs, alpha, nj, n_p, split_in, split_out):
    n_x = 2 if split_in else 1
    n_o = 2 if split_out else 1
    x_refs = refs[:n_x]
    wg_ref, wu_ref, wd_ref, g_ref, b_ref = refs[n_x:n_x + 5]
    o_refs = refs[n_x + 5:n_x + 5 + n_o]
    xb_ref, acc_ref = refs[n_x + 5 + n_o:]
    i, j = pl.program_id(0), pl.program_id(1)
    groups = (i < n_p, i >= n_p)

    def start(x_ref):
        xb_ref[...] = x_ref[...].astype(BF16)
        acc_ref[...] = jnp.zeros_like(acc_ref)

    def finish(x_ref, o_ref):
        y = alpha * x_ref[...] + 0.5 * acc_ref[...]
        o_ref[...] = _layer_norm(y, g_ref[...], b_ref[...])

    if split_in:
        for x_ref, grp in zip(x_refs, groups):
            pl.when((j == 0) & grp)(functools.partial(start, x_ref))
    else:
        pl.when(j == 0)(functools.partial(start, x_refs[0]))

    xb = xb_ref[...]
    gate = jnp.dot(xb, wg_ref[...], preferred_element_type=F32)
    up = jnp.dot(xb, wu_ref[...], preferred_element_type=F32)
    h = (gate * jax.nn.sigmoid(gate)) * up
    acc_ref[...] += jnp.dot(h.astype(BF16), wd_ref[...], preferred_element_type=F32)

    last = j == nj - 1
    if split_in or split_out:
        for g_idx, grp in enumerate(groups):
            x_ref = x_refs[g_idx] if split_in else x_refs[0]
            o_ref = o_refs[g_idx] if split_out else o_refs[0]
            pl.when(last & grp)(functools.partial(finish, x_ref, o_ref))
    else:
        pl.when(last)(functools.partial(finish, x_refs[0], o_refs[0]))


def _ffn(xs, wg, wu, wd, g, b, layer, alpha, n_p_rows, split_out=False, tm=512, tf=512):
    split_in = len(xs) == 2
    d = xs[0].shape[1]
    t = sum(x.shape[0] for x in xs)
    f = wg.shape[-1]
    nj = f // tf
    assert t % tm == 0 and f % tf == 0 and n_p_rows % tm == 0
    n_p = n_p_rows // tm

    def grouped_specs():
        return [pl.BlockSpec((tm, d), lambda i, j: (jnp.minimum(i, n_p - 1), 0)),
                pl.BlockSpec((tm, d), lambda i, j: (jnp.maximum(i - n_p, 0), 0))]

    flat_spec = [pl.BlockSpec((tm, d), lambda i, j: (i, 0))]
    if split_out:
        out_shape = [jax.ShapeDtypeStruct((n_p_rows, d), F32), jax.ShapeDtypeStruct((t - n_p_rows, d), F32)]
    else:
        out_shape = [jax.ShapeDtypeStruct((t, d), F32)]
    return pl.pallas_call(
        functools.partial(_ffn_kernel, alpha=alpha, nj=nj, n_p=n_p, split_in=split_in, split_out=split_out),
        out_shape=out_shape,
        grid=(t // tm, nj),
        in_specs=(grouped_specs() if split_in else flat_spec) + [
            pl.BlockSpec((None, d, tf), lambda i, j: (layer, 0, j)),
            pl.BlockSpec((None, d, tf), lambda i, j: (layer, 0, j)),
            pl.BlockSpec((None, tf, d), lambda i, j: (layer, j, 0)),
            pl.BlockSpec((None, 1, d), lambda i, j: (layer, 0, 0)),
            pl.BlockSpec((None, 1, d), lambda i, j: (layer, 0, 0)),
        ],
        out_specs=grouped_specs() if split_out else flat_spec,
        scratch_shapes=[pltpu.VMEM((tm, d), BF16), pltpu.VMEM((tm, d), F32)],
        compiler_params=_params("arbitrary", "arbitrary"),
        name="ffn",
    )(*xs, wg, wu, wd, g, b)


def _rotary(t, cos_t, sin_lo, sin_hi):
    half = ROT_DIM // 2
    out = []
    for p in range(HEAD_PAIRS):
        tp = t[:, p * LANES:(p + 1) * LANES]
        up = pltpu.roll(tp, LANES - half, 1)
        dn = pltpu.roll(tp, half, 1)
        out.append(tp * cos_t + up * sin_lo + dn * sin_hi)
    return jnp.concatenate(out, axis=1)


def _emit_dilated(t, out_refs, stage_ref):
    rows = t.shape[0]
    for dil, out_ref in zip(DILATIONS, out_refs):
        if dil == 1:
            out_ref[...] = t.astype(BF16)
    for p in range(HEAD_PAIRS):
        stage_ref[p] = t[:, p * LANES:(p + 1) * LANES]
    for dil, out_ref in zip(DILATIONS, out_refs):
        if dil == 1:
            continue
        for r in range(dil):
            for p in range(HEAD_PAIRS):
                piece = stage_ref[p, pl.ds(r, rows // dil, stride=dil), :]
                c0 = r * ATT_WIDTH + p * LANES
                out_ref[:, c0:c0 + LANES] = piece.astype(BF16)


def _inproj_kernel(x_ref, wq_ref, wk_ref, wv_ref, wx_ref, wg_ref, cos_ref, slo_ref, shi_ref, *rest):
    n_pat = len(DILATIONS)
    q_refs, k_refs, v_refs = rest[:n_pat], rest[n_pat:2 * n_pat], rest[2 * n_pat:3 * n_pat]
    xl_ref, gl_ref, stage_ref = rest[3 * n_pat:]
    xb = x_ref[...].astype(BF16)
    cos_t, sin_lo, sin_hi = cos_ref[...], slo_ref[...], shi_ref[...]
    q = jnp.dot(xb, wq_ref[...], preferred_element_type=F32)
    _emit_dilated(_rotary(q, cos_t, sin_lo, sin_hi) * (HEAD_DIM ** -0.5), q_refs, stage_ref)
    k = jnp.dot(xb, wk_ref[...], preferred_element_type=F32)
    _emit_dilated(_rotary(k, cos_t, sin_lo, sin_hi), k_refs, stage_ref)
    _emit_dilated(jnp.dot(xb, wv_ref[...], preferred_element_type=F32), v_refs, stage_ref)
    xl_ref[...] = jnp.dot(xb, wx_ref[...], preferred_element_type=F32)
    gl_ref[...] = jnp.dot(xb, wg_ref[...], preferred_element_type=F32)


def _rope_tables(n_pos):
    half = ROT_DIM // 2
    inv_freq = ROPE_THETA ** (-jnp.arange(0, ROT_DIM, 2, dtype=F32) / ROT_DIM)
    ang = jnp.arange(n_pos, dtype=F32)[:, None] * inv_freq[None, :]
    cos, sin = jnp.cos(ang), jnp.sin(ang)
    rest = HEAD_DIM - ROT_DIM
    one = jnp.ones((n_pos, rest), F32)
    zero = jnp.zeros((n_pos, rest), F32)
    zh = jnp.zeros((n_pos, half), F32)
    reps = LANES // HEAD_DIM
    cos_t = jnp.tile(jnp.concatenate([cos, cos, one], axis=1), (1, reps))
    sin_lo = jnp.tile(jnp.concatenate([-sin, zh, zero], axis=1), (1, reps))
    sin_hi = jnp.tile(jnp.concatenate([zh, sin, zero], axis=1), (1, reps))
    return cos_t, sin_lo, sin_hi


def _inproj(x, w_qkv, w_lru, tables, layer, geom, tm=256):
    t, d = x.shape
    rows_p, len_p, len_s = geom
    assert t % tm == 0 and len_p % tm == 0 and len_s % tm == 0
    n_p, bp, bs = rows_p // tm, len_p // tm, len_s // tm

    def pos_map(i):
        return (jnp.where(i < n_p, lax.rem(i, bp), lax.rem(jnp.maximum(i - n_p, 0), bs)), 0)

    def wspec(width, col):
        return pl.BlockSpec((None, d, width), lambda i: (layer, 0, col), pipeline_mode=pl.Buffered(1))

    tab_spec = pl.BlockSpec((tm, LANES), pos_map)
    att_shapes = [jax.ShapeDtypeStruct((t // dil, dil * ATT_WIDTH), BF16) for dil in DILATIONS]
    att_specs = [pl.BlockSpec((tm // dil, dil * ATT_WIDTH), lambda i: (i, 0)) for dil in DILATIONS]
    lru_spec = pl.BlockSpec((tm, LRU_WIDTH), lambda i: (i, 0))
    outs = pl.pallas_call(
        _inproj_kernel,
        out_shape=att_shapes * 3 + [jax.ShapeDtypeStruct((t, LRU_WIDTH), F32)] * 2,
        grid=(t // tm,),
        in_specs=[
            pl.BlockSpec((tm, d), lambda i: (i, 0)),
            wspec(ATT_WIDTH, 0), wspec(ATT_WIDTH, 1), wspec(ATT_WIDTH, 2),
            wspec(LRU_WIDTH, 0), wspec(LRU_WIDTH, 1),
            tab_spec, tab_spec, tab_spec,
        ],
        out_specs=att_specs * 3 + [lru_spec, lru_spec],
        scratch_shapes=[pltpu.VMEM((HEAD_PAIRS, tm, LANES), F32)],
        compiler_params=_params("parallel"),
        name="inproj",
    )(x, w_qkv, w_qkv, w_qkv, w_lru, w_lru, *tables)
    n_pat = len(DILATIONS)
    return outs[:n_pat], outs[n_pat:2 * n_pat], outs[2 * n_pat:3 * n_pat], outs[-2], outs[-1]


def _attn_kernel(q_ref, km_ref, kl_ref, kr_ref, vm_ref, vl_ref, vr_ref, o_ref, lse_ref,
                 kc_ref, vc_ref, *, tq, geom):
    rows_p, len_p, len_s = geom
    row0 = pl.program_id(0) * tq
    lo, hi = _seq_bounds(row0, rows_p, len_p, len_s)

    kc_ref[0:RADIUS] = kl_ref[...]
    kc_ref[RADIUS:RADIUS + tq] = km_ref[...]
    kc_ref[RADIUS + tq:] = kr_ref[...]
    vc_ref[0:RADIUS] = vl_ref[...]
    vc_ref[RADIUS:RADIUS + tq] = vm_ref[...]
    vc_ref[RADIUS + tq:] = vr_ref[...]

    qb_rows = LANES
    kb_rows = qb_rows + 2 * RADIUS
    lane = lax.broadcasted_iota(jnp.int32, (qb_rows, LANES), 1)
    low_half = lane < HEAD_DIM
    trow = lax.broadcasted_iota(jnp.int32, (qb_rows, kb_rows), 0)
    ccol = lax.broadcasted_iota(jnp.int32, (qb_rows, kb_rows), 1)
    band = (ccol >= trow) & (ccol <= trow + 2 * RADIUS)

    for qb in range(tq // qb_rows):
        r0 = qb * qb_rows
        base = row0 + r0 - RADIUS
        valid = band & (ccol >= lo - base) & (ccol < hi - base)
        lse_tile = jnp.zeros((qb_rows, LANES), F32)
        for p in range(HEAD_PAIRS):
            cols = slice(p * LANES, (p + 1) * LANES)
            qp = q_ref[r0:r0 + qb_rows, cols]
            kp = kc_ref[r0:r0 + kb_rows, cols]
            vp = vc_ref[r0:r0 + kb_rows, cols]
            outs = []
            for hh in range(2):
                sel = low_half if hh == 0 else jnp.logical_not(low_half)
                qh = jnp.where(sel, qp, jnp.zeros_like(qp))
                s = lax.dot_general(qh, kp, (((1,), (1,)), ((), ())), preferred_element_type=F32)
                s = jnp.where(valid, s, NEG_INF)
                m = jnp.max(s, axis=1, keepdims=True)
                e = jnp.exp(s - m)
                l = jnp.sum(e, axis=1, keepdims=True)
                pv = jnp.dot(e.astype(BF16), vp, preferred_element_type=F32)
                outs.append(pv / l)
                lse_tile = jnp.where(lane == 2 * p + hh, m + jnp.log(l), lse_tile)
            o_ref[r0:r0 + qb_rows, cols] = jnp.where(low_half, outs[0], outs[1])
        lse_ref[r0:r0 + qb_rows, :] = lse_tile


def _attn_pattern(q, k, v, dil, geom, max_tq=512):
    rows = q.shape[0]
    rows_p, len_p, len_s = (g // dil for g in geom)
    tq = min(max_tq, len_p, len_s)
    assert rows % tq == 0 and len_p % tq == 0 and len_s % tq == 0 and tq % RADIUS == 0
    hb = tq // RADIUS
    n_halo = rows // RADIUS
    main = pl.BlockSpec((tq, ATT_WIDTH), lambda i, r: (i, r))
    left = pl.BlockSpec((RADIUS, ATT_WIDTH), lambda i, r: (jnp.maximum(i * hb - 1, 0), r))
    right = pl.BlockSpec((RADIUS, ATT_WIDTH), lambda i, r: (jnp.minimum((i + 1) * hb, n_halo - 1), r))
    return pl.pallas_call(
        functools.partial(_attn_kernel, tq=tq, geom=(rows_p, len_p, len_s)),
        out_shape=[jax.ShapeDtypeStruct((rows, dil * ATT_WIDTH), F32),
                   jax.ShapeDtypeStruct((rows, dil * LANES), F32)],
        grid=(rows // tq, dil),
        in_specs=[main, main, left, right, main, left, right],
        out_specs=[pl.BlockSpec((tq, ATT_WIDTH), lambda i, r: (i, r)),
                   pl.BlockSpec((tq, LANES), lambda i, r: (i, r))],
        scratch_shapes=[pltpu.VMEM((tq + 2 * RADIUS, ATT_WIDTH), BF16),
                        pltpu.VMEM((tq + 2 * RADIUS, ATT_WIDTH), BF16)],
        compiler_params=_params("parallel", "parallel"),
        name=f"attn_d{dil}",
    )(q, k, k, k, v, v, v)


def _sublane_scan(h_end, p_end, carry, sub, reverse):
    n = SUBLANES
    k = 1
    while k < n:
        if reverse:
            keep, shift = sub < n - k, n - k
        else:
            keep, shift = sub >= k, k
        h_sh = jnp.where(keep, pltpu.roll(h_end, shift, 0), 0.0)
        p_sh = jnp.where(keep, pltpu.roll(p_end, shift, 0), 1.0)
        h_end = p_end * h_sh + h_end
        p_end = p_end * p_sh
        k *= 2
    leave = h_end + p_end * carry
    if reverse:
        enter = jnp.where(sub < n - 1, pltpu.roll(leave, n - 1, 0), carry)
        return enter, leave[0:1]
    enter = jnp.where(sub >= 1, pltpu.roll(leave, 1, 0), carry)
    return enter, leave[n - 1:n]


def _lru_kernel(xm_ref, xp_ref, xn_ref, cw_ref, cb_ref, w_ref, ba_ref, bx_ref, lam_ref, h_ref,
                carry_ref, xs_ref, hs_ref, *, tc, nc, reverse, geom):
    rows_p, len_p, len_s = geom
    c = pl.program_id(0)
    row0 = ((nc - 1 - c) if reverse else c) * tc
    lo, hi = _seq_bounds(row0, rows_p, len_p, len_s)
    at_start = row0 == lo
    at_end = row0 + tc == hi

    @pl.when(at_end if reverse else at_start)
    def _():
        carry_ref[...] = jnp.zeros_like(carry_ref)

    halo = xp_ref.shape[0]
    steps = tc // SUBLANES
    pitch = steps + SCAN_PITCH_PAD
    conv_right = CONV_WIDTH - 1 - CONV_LEFT
    assert CONV_WIDTH - 1 <= SCAN_PITCH_PAD and CONV_LEFT <= halo and conv_right <= halo
    for n in range(LRU_BLOCKS):
        cols = slice(n * LRU_BLOCK, (n + 1) * LRU_BLOCK)
        for j in range(SUBLANES):
            r0 = steps * j
            xs_ref[n, pitch * j + CONV_LEFT:pitch * j + CONV_LEFT + steps, :] = xm_ref[r0:r0 + steps, cols]
            if j == 0:
                left = jnp.where(at_start, 0.0, xp_ref[halo - CONV_LEFT:halo, cols])
            else:
                left = xm_ref[r0 - CONV_LEFT:r0, cols]
            xs_ref[n, pitch * j:pitch * j + CONV_LEFT, :] = left
            if j == SUBLANES - 1:
                right = jnp.where(at_end, 0.0, xn_ref[0:conv_right, cols])
            else:
                right = xm_ref[r0 + steps:r0 + steps + conv_right, cols]
            r1 = pitch * j + CONV_LEFT + steps
            xs_ref[n, r1:r1 + conv_right, :] = right

    cw = cw_ref[...]
    cb = cb_ref[...]
    sub = lax.broadcasted_iota(jnp.int32, (SUBLANES, LRU_BLOCK), 0)
    neg = -lam_ref[...]
    softplus = jnp.maximum(neg, 0.0) + jnp.log1p(jnp.exp(-jnp.abs(neg)))
    cl = -LRU_C * softplus
    order = range(steps - 1, -1, -1) if reverse else range(steps)
    for n in range(LRU_BLOCKS):
        cols = slice(n * LRU_BLOCK, (n + 1) * LRU_BLOCK)
        taps = [jnp.concatenate([xs_ref[n, pl.ds(s + tap, SUBLANES, stride=pitch), :] for s in range(steps)],
                                axis=0) * cw[tap:tap + 1, cols] for tap in range(CONV_WIDTH)]
        xq = functools.reduce(lambda x, y: x + y, taps) + cb[:, cols]
        gates = jnp.dot(xq.astype(BF16), w_ref[n], preferred_element_type=F32)
        r = _sigmoid(gates[:, :LRU_BLOCK] + ba_ref[:, cols])
        i = _sigmoid(gates[:, LRU_BLOCK:] + bx_ref[:, cols])
        a = jnp.exp(cl[:, cols] * r)
        u = jnp.sqrt(1.0 - a * a) * i * xq

        h = jnp.zeros((SUBLANES, LRU_BLOCK), F32)
        pr = jnp.ones((SUBLANES, LRU_BLOCK), F32)
        h_loc, p_loc = {}, {}
        for s in order:
            a_s = a[SUBLANES * s:SUBLANES * (s + 1)]
            h = a_s * h + u[SUBLANES * s:SUBLANES * (s + 1)]
            pr = a_s * pr
            h_loc[s], p_loc[s] = h, pr
        enter, leave = _sublane_scan(h, pr, carry_ref[:, cols], sub, reverse)
        carry_ref[:, cols] = leave
        for s in range(steps):
            hs_ref[n, pl.ds(s, SUBLANES, stride=pitch), :] = h_loc[s] + p_loc[s] * enter

    for j in range(SUBLANES):
        for n in range(LRU_BLOCKS):
            h_ref[steps * j:steps * (j + 1), n * LRU_BLOCK:(n + 1) * LRU_BLOCK] = (
                hs_ref[n, pitch * j:pitch * j + steps, :])


def _lru_direction(xl, cw, cb, w_gate, ba, bx, lam, layer, direction, geom, tc=512, halo=8):
    t = xl.shape[0]
    nc = t // tc
    rows_p, len_p, len_s = geom
    assert t % tc == 0 and len_p % tc == 0 and len_s % tc == 0 and tc % halo == 0
    reverse = direction == 1
    hb = tc // halo
    n_halo = t // halo
    stage_rows = SUBLANES * (tc // SUBLANES + SCAN_PITCH_PAD)

    def blk(c):
        return (nc - 1 - c) if reverse else c

    vec = pl.BlockSpec((None, None, 1, LRU_WIDTH), lambda c: (layer, direction, 0, 0))
    return pl.pallas_call(
        functools.partial(_lru_kernel, tc=tc, nc=nc, reverse=reverse, geom=geom),
        out_shape=jax.ShapeDtypeStruct((t, LRU_WIDTH), F32),
        grid=(nc,),
        in_specs=[
            pl.BlockSpec((tc, LRU_WIDTH), lambda c: (blk(c), 0)),
            pl.BlockSpec((halo, LRU_WIDTH), lambda c: (jnp.maximum(blk(c) * hb - 1, 0), 0)),
            pl.BlockSpec((halo, LRU_WIDTH), lambda c: (jnp.minimum((blk(c) + 1) * hb, n_halo - 1), 0)),
            pl.BlockSpec((None, CONV_WIDTH, LRU_WIDTH), lambda c: (layer, 0, 0)),
            pl.BlockSpec((None, 1, LRU_WIDTH), lambda c: (layer, 0, 0)),
            pl.BlockSpec((None, None, LRU_BLOCKS, LRU_BLOCK, 2 * LRU_BLOCK),
                         lambda c: (layer, direction, 0, 0, 0)),
            vec, vec, vec,
        ],
        out_specs=pl.BlockSpec((tc, LRU_WIDTH), lambda c: (blk(c), 0)),
        scratch_shapes=[pltpu.VMEM((1, LRU_WIDTH), F32),
                        pltpu.VMEM((LRU_BLOCKS, stage_rows, LRU_BLOCK), F32),
                        pltpu.VMEM((LRU_BLOCKS, stage_rows, LRU_BLOCK), F32)],
        compiler_params=_params("arbitrary"),
        name="lru_bwd" if reverse else "lru_fwd",
    )(xl, xl, xl, cw, cb, w_gate, ba, bx, lam)


def _merge_kernel(*refs, alpha, sub_rows):
    n_pat = len(DILATIONS)
    x_ref = refs[0]
    o_refs = refs[1:1 + n_pat]
    l_refs = refs[1 + n_pat:1 + 2 * n_pat]
    hf_ref, hb_ref, gl_ref, ag_ref, lg_ref, wo_ref, g_ref, b_ref, out_ref, m_ref, os_ref, ls_ref = refs[1 + 2 * n_pat:]
    tm = x_ref.shape[0]

    for g, dil in enumerate(DILATIONS):
        if dil == 1:
            continue
        for r in range(dil):
            ls_ref[g, pl.ds(r, tm // dil, stride=dil), :] = l_refs[g][:, r * LANES:(r + 1) * LANES]
            for p in range(HEAD_PAIRS):
                c0 = r * ATT_WIDTH + p * LANES
                os_ref[g, p, pl.ds(r, tm // dil, stride=dil), :] = o_refs[g][:, c0:c0 + LANES]

    def lse_of(g, rs):
        return l_refs[g][rs, :] if DILATIONS[g] == 1 else ls_ref[g, rs, :]

    def out_of(g, p, rs):
        if DILATIONS[g] == 1:
            return o_refs[g][rs, p * LANES:(p + 1) * LANES]
        return os_ref[g, p, rs, :]

    low_half = lax.broadcasted_iota(jnp.int32, (sub_rows, LANES), 1) < HEAD_DIM
    row_sets = [slice(h * sub_rows, (h + 1) * sub_rows) for h in range(tm // sub_rows)]
    for rs in row_sets:
        lses = [lse_of(g, rs) for g in range(n_pat)]
        mx = functools.reduce(jnp.maximum, lses)
        es = [jnp.exp(l - mx) for l in lses]
        den = functools.reduce(lambda a, b: a + b, es)
        ws = [e / den for e in es]
        pieces = []
        for p in range(HEAD_PAIRS):
            acc = None
            for g, w in enumerate(ws):
                wsel = jnp.where(low_half, w[:, 2 * p:2 * p + 1], w[:, 2 * p + 1:2 * p + 2])
                term = out_of(g, p, rs) * wsel
                acc = term if acc is None else acc + term
            pieces.append(acc)
        attn = jnp.concatenate(pieces, axis=1)
        m_ref[rs, :ATT_WIDTH] = _rms_norm(attn, ag_ref[...]).astype(BF16)
        rec = (hf_ref[rs, :] + hb_ref[rs, :]) * jax.nn.gelu(gl_ref[rs, :])
        m_ref[rs, ATT_WIDTH:] = _rms_norm(rec, lg_ref[...]).astype(BF16)

    for rs in row_sets:
        mix = jnp.dot(m_ref[rs, :], wo_ref[...], preferred_element_type=F32)
        out_ref[rs, :] = _layer_norm(alpha * x_ref[rs, :] + mix, g_ref[...], b_ref[...])


def _merge(x, os_, lses, hf, hb, gl, ag, lg, wo, g, b, layer, alpha, tm=256, sub_rows=128):
    t, d = x.shape
    assert t % tm == 0 and tm % sub_rows == 0
    n_pat = len(DILATIONS)

    def rows(width, dil=1):
        return pl.BlockSpec((tm // dil, dil * width), lambda i: (i, 0))

    def vec(width):
        return pl.BlockSpec((None, 1, width), lambda i: (layer, 0, 0))

    return pl.pallas_call(
        functools.partial(_merge_kernel, alpha=alpha, sub_rows=sub_rows),
        out_shape=jax.ShapeDtypeStruct((t, d), F32),
        grid=(t // tm,),
        in_specs=[rows(d)] + [rows(ATT_WIDTH, dil) for dil in DILATIONS]
        + [rows(LANES, dil) for dil in DILATIONS] + [rows(LRU_WIDTH)] * 3
        + [vec(ATT_WIDTH), vec(LRU_WIDTH),
           pl.BlockSpec((None, d, d), lambda i: (layer, 0, 0), pipeline_mode=pl.Buffered(1)),
           vec(d), vec(d)],
        out_specs=rows(d),
        scratch_shapes=[pltpu.VMEM((tm, d), BF16),
                        pltpu.VMEM((n_pat, HEAD_PAIRS, tm, LANES), F32),
                        pltpu.VMEM((n_pat, tm, LANES), F32)],
        compiler_params=_params("parallel"),
        name="merge",
    )(x, *os_, *lses, hf, hb, gl, ag, lg, wo, g, b)


def kernel(x_prompt, x_sample, ln1_g, ln1_b, ffn1_w_gate, ffn1_w_up, ffn1_w_down, ln2_g, ln2_b, w_in, conv_w, conv_b, lru_w_a, lru_b_a, lru_w_x, lru_b_x, lru_lambda, att_norm_g, lru_norm_g, w_out, ln3_g, ln3_b, ffn2_w_gate, ffn2_w_up, ffn2_w_down):
    nb, s, d = x_prompt.shape
    db, ds, _ = x_sample.shape
    depth = ln1_g.shape[0]
    alpha = (2 * depth) ** 0.25
    assert d == ATT_WIDTH + LRU_WIDTH
    geom = (nb * s, s, ds)

    tables = _rope_tables(max(s, ds))
    row3 = lambda a: a.reshape(depth, 1, a.shape[-1])
    ln1 = (row3(ln1_g), row3(ln1_b))
    ln2 = (row3(ln2_g), row3(ln2_b))
    ln3 = (row3(ln3_g), row3(ln3_b))
    ffn1 = tuple(w.astype(BF16) for w in (ffn1_w_gate, ffn1_w_up, ffn1_w_down))
    ffn2 = tuple(w.astype(BF16) for w in (ffn2_w_gate, ffn2_w_up, ffn2_w_down))
    w_in_b = w_in.astype(BF16)
    w_qkv, w_lru = w_in_b[:, :, :3 * ATT_WIDTH], w_in_b[:, :, 3 * ATT_WIDTH:]
    w_gate = jnp.concatenate([lru_w_a, lru_w_x], axis=-1).astype(BF16)
    vec4 = lambda a: a.reshape(depth, 2, 1, LRU_WIDTH)
    ba, bx, lam = vec4(lru_b_a), vec4(lru_b_x), vec4(lru_lambda)
    cb = row3(conv_b)
    ag, lg = row3(att_norm_g), row3(lru_norm_g)
    w_out_b = w_out.astype(BF16)

    xs = (x_prompt.reshape(nb * s, d), x_sample.reshape(db * ds, d))
    for layer in range(depth):
        (x,) = _ffn(xs, *ffn1, *ln1, layer, alpha, nb * s)
        qs, ks, vs, xl, gl = _inproj(x, w_qkv, w_lru, tables, layer, geom)
        os_, lses = [], []
        for g, dil in enumerate(DILATIONS):
            o, lse = _attn_pattern(qs[g], ks[g], vs[g], dil, geom)
            os_.append(o)
            lses.append(lse)
        hf = _lru_direction(xl, conv_w, cb, w_gate, ba, bx, lam, layer, 0, geom)
        hb = _lru_direction(xl, conv_w, cb, w_gate, ba, bx, lam, layer, 1, geom)
        x = _merge(x, os_, lses, hf, hb, gl, ag, lg, w_out_b, *ln2, layer, alpha)
        xs = _ffn((x,), *ffn2, *ln3, layer, alpha, nb * s, split_out=layer == depth - 1)

    y_prompt, y_sample = xs
    return (y_prompt.reshape(nb, s, d), y_sample.reshape(db, ds, d))
```

```python
import functools

import jax
import jax.numpy as jnp
from jax import lax
from jax.experimental import pallas as pl
from jax.experimental.pallas import tpu as pltpu

HEAD_DIM = 64
ATT_HEADS = 12
ATT_WIDTH = ATT_HEADS * HEAD_DIM
LRU_BLOCKS = 10
LRU_BLOCK = 128
LRU_WIDTH = LRU_BLOCKS * LRU_BLOCK
CONV_WIDTH = 4
CONV_LEFT = 2
LRU_C = 8.0
ROPE_THETA = 500000.0
ROT_DIM = HEAD_DIM // 4
DILATED_PATTERNS = ((128, 1), (512, 4), (2048, 16))
DILATIONS = tuple(d for _, d in DILATED_PATTERNS)
RADIUS = 64
assert all(w // (2 * d) == RADIUS for w, d in DILATED_PATTERNS)
NORM_EPS = 1e-5
NEG_INF = -1e30
LN2 = 0.6931471805599453
LOG2E = 1.4426950408889634

LANES = 128
SUBLANES = 8
HEAD_PAIRS = ATT_WIDTH // LANES
VMEM_LIMIT_BYTES = 56 * 1024 * 1024
SCAN_PITCH_PAD = 4

F32 = jnp.float32
BF16 = jnp.bfloat16


def _params(*semantics):
    return pltpu.CompilerParams(dimension_semantics=semantics, vmem_limit_bytes=VMEM_LIMIT_BYTES)


def _layer_norm(y, g, b):
    mu = jnp.mean(y, axis=-1, keepdims=True)
    yc = y - mu
    var = jnp.mean(yc * yc, axis=-1, keepdims=True)
    return yc * lax.rsqrt(var + NORM_EPS) * g + b


def _rms_norm(y, g):
    ms = jnp.mean(y * y, axis=-1, keepdims=True)
    return y * lax.rsqrt(ms + NORM_EPS) * g


def _sigmoid(z):
    return 0.5 * jnp.tanh(0.5 * z) + 0.5


def _seq_bounds(row0, rows_p, len_p, len_s):
    in_p = row0 < rows_p
    lo_p = lax.div(row0, len_p) * len_p
    lo_s = rows_p + lax.div(jnp.maximum(row0 - rows_p, 0), len_s) * len_s
    lo = jnp.where(in_p, lo_p, lo_s)
    hi = jnp.where(in_p, lo_p + len_p, lo_s + len_s)
    return lo, hi


def _ffn_kernel(*refs, alpha, nj, n_p, split_in, split_out):
    n_x = 2 if split_in else 1
    n_o = 2 if split_out else 1
    x_refs = refs[:n_x]
    wg_ref, wu_ref, wd_ref, g_ref, b_ref = refs[n_x:n_x + 5]
    o_refs = refs[n_x + 5:n_x + 5 + n_o]
    xb_ref, acc_ref = refs[n_x + 5 + n_o:]
    i, j = pl.program_id(0), pl.program_id(1)
    groups = (i < n_p, i >= n_p)

    def start(x_ref):
        xb_ref[...] = x_ref[...].astype(BF16)
        acc_ref[...] = jnp.zeros_like(acc_ref)

    def finish(x_ref, o_ref):
        y = alpha * x_ref[...] + 0.5 * acc_ref[...]
        o_ref[...] = _layer_norm(y, g_ref[...], b_ref[...])

    if split_in:
        for x_ref, grp in zip(x_refs, groups):
            pl.when((j == 0) & grp)(functools.partial(start, x_ref))
    else:
        pl.when(j == 0)(functools.partial(start, x_refs[0]))

    xb = xb_ref[...]
    gate = jnp.dot(xb, wg_ref[...], preferred_element_type=F32)
    up = jnp.dot(xb, wu_ref[...], preferred_element_type=F32)
    h = (gate * jax.nn.sigmoid(gate)) * up
    acc_ref[...] += jnp.dot(h.astype(BF16), wd_ref[...], preferred_element_type=F32)

    last = j == nj - 1
    if split_in or split_out:
        for g_idx, grp in enumerate(groups):
            x_ref = x_refs[g_idx] if split_in else x_refs[0]
            o_ref = o_refs[g_idx] if split_out else o_refs[0]
            pl.when(last & grp)(functools.partial(finish, x_ref, o_ref))
    else:
        pl.when(last)(functools.partial(finish, x_refs[0], o_refs[0]))


def _ffn(xs, wg, wu, wd, g, b, layer, alpha, n_p_rows, split_out=False, tm=512, tf=512):
    split_in = len(xs) == 2
    d = xs[0].shape[1]
    t = sum(x.shape[0] for x in xs)
    f = wg.shape[-1]
    nj = f // tf
    assert t % tm == 0 and f % tf == 0 and n_p_rows % tm == 0
    n_p = n_p_rows // tm

    def grouped_specs():
        return [pl.BlockSpec((tm, d), lambda i, j: (jnp.minimum(i, n_p - 1), 0)),
                pl.BlockSpec((tm, d), lambda i, j: (jnp.maximum(i - n_p, 0), 0))]

    flat_spec = [pl.BlockSpec((tm, d), lambda i, j: (i, 0))]
    if split_out:
        out_shape = [jax.ShapeDtypeStruct((n_p_rows, d), F32), jax.ShapeDtypeStruct((t - n_p_rows, d), F32)]
    else:
        out_shape = [jax.ShapeDtypeStruct((t, d), F32)]
    return pl.pallas_call(
        functools.partial(_ffn_kernel, alpha=alpha, nj=nj, n_p=n_p, split_in=split_in, split_out=split_out),
        out_shape=out_shape,
        grid=(t // tm, nj),
        in_specs=(grouped_specs() if split_in else flat_spec) + [
            pl.BlockSpec((None, d, tf), lambda i, j: (layer, 0, j)),
            pl.BlockSpec((None, d, tf), lambda i, j: (layer, 0, j)),
            pl.BlockSpec((None, tf, d), lambda i, j: (layer, j, 0)),
            pl.BlockSpec((None, 1, d), lambda i, j: (layer, 0, 0)),
            pl.BlockSpec((None, 1, d), lambda i, j: (layer, 0, 0)),
        ],
        out_specs=grouped_specs() if split_out else flat_spec,
        scratch_shapes=[pltpu.VMEM((tm, d), BF16), pltpu.VMEM((tm, d), F32)],
        compiler_params=_params("arbitrary", "arbitrary"),
        name="ffn",
    )(*xs, wg, wu, wd, g, b)


def _rotary(t, cos_t, sin_lo, sin_hi):
    half = ROT_DIM // 2
    out = []
    for p in range(HEAD_PAIRS):
        tp = t[:, p * LANES:(p + 1) * LANES]
        up = pltpu.roll(tp, LANES - half, 1)
        dn = pltpu.roll(tp, half, 1)
        out.append(tp * cos_t + up * sin_lo + dn * sin_hi)
    return jnp.concatenate(out, axis=1)


def _emit_dilated(t, out_refs, stage_ref):
    rows = t.shape[0]
    for dil, out_ref in zip(DILATIONS, out_refs):
        if dil == 1:
            out_ref[...] = t.astype(BF16)
    for p in range(HEAD_PAIRS):
        stage_ref[p] = t[:, p * LANES:(p + 1) * LANES]
    for dil, out_ref in zip(DILATIONS, out_refs):
        if dil == 1:
            continue
        for r in range(dil):
            for p in range(HEAD_PAIRS):
                piece = stage_ref[p, pl.ds(r, rows // dil, stride=dil), :]
                c0 = r * ATT_WIDTH + p * LANES
                out_ref[:, c0:c0 + LANES] = piece.astype(BF16)


def _inproj_kernel(x_ref, wq_ref, wk_ref, wv_ref, wx_ref, wg_ref, cos_ref, slo_ref, shi_ref, *rest):
    n_pat = len(DILATIONS)
    q_refs, k_refs, v_refs = rest[:n_pat], rest[n_pat:2 * n_pat], rest[2 * n_pat:3 * n_pat]
    xl_ref, gl_ref, stage_ref = rest[3 * n_pat:]
    xb = x_ref[...].astype(BF16)
    cos_t, sin_lo, sin_hi = cos_ref[...], slo_ref[...], shi_ref[...]
    q = jnp.dot(xb, wq_ref[...], preferred_element_type=F32)
    _emit_dilated(_rotary(q, cos_t, sin_lo, sin_hi) * (HEAD_DIM ** -0.5 * LOG2E), q_refs, stage_ref)
    k = jnp.dot(xb, wk_ref[...], preferred_element_type=F32)
    _emit_dilated(_rotary(k, cos_t, sin_lo, sin_hi), k_refs, stage_ref)
    _emit_dilated(jnp.dot(xb, wv_ref[...], preferred_element_type=F32), v_refs, stage_ref)
    xl_ref[...] = jnp.dot(xb, wx_ref[...], preferred_element_type=F32)
    gl_ref[...] = jnp.dot(xb, wg_ref[...], preferred_element_type=F32)


def _rope_tables(n_pos):
    half = ROT_DIM // 2
    inv_freq = ROPE_THETA ** (-jnp.arange(0, ROT_DIM, 2, dtype=F32) / ROT_DIM)
    ang = jnp.arange(n_pos, dtype=F32)[:, None] * inv_freq[None, :]
    cos, sin = jnp.cos(ang), jnp.sin(ang)
    rest = HEAD_DIM - ROT_DIM
    one = jnp.ones((n_pos, rest), F32)
    zero = jnp.zeros((n_pos, rest), F32)
    zh = jnp.zeros((n_pos, half), F32)
    reps = LANES // HEAD_DIM
    cos_t = jnp.tile(jnp.concatenate([cos, cos, one], axis=1), (1, reps))
    sin_lo = jnp.tile(jnp.concatenate([-sin, zh, zero], axis=1), (1, reps))
    sin_hi = jnp.tile(jnp.concatenate([zh, sin, zero], axis=1), (1, reps))
    return cos_t, sin_lo, sin_hi


def _inproj(x, w_qkv, w_lru, tables, layer, geom, tm=256):
    t, d = x.shape
    rows_p, len_p, len_s = geom
    assert t % tm == 0 and len_p % tm == 0 and len_s % tm == 0
    n_p, bp, bs = rows_p // tm, len_p // tm, len_s // tm

    def pos_map(i):
        return (jnp.where(i < n_p, lax.rem(i, bp), lax.rem(jnp.maximum(i - n_p, 0), bs)), 0)

    def wspec(width, col):
        return pl.BlockSpec((None, d, width), lambda i: (layer, 0, col), pipeline_mode=pl.Buffered(1))

    tab_spec = pl.BlockSpec((tm, LANES), pos_map)
    att_shapes = [jax.ShapeDtypeStruct((t // dil, dil * ATT_WIDTH), BF16) for dil in DILATIONS]
    att_specs = [pl.BlockSpec((tm // dil, dil * ATT_WIDTH), lambda i: (i, 0)) for dil in DILATIONS]
    lru_spec = pl.BlockSpec((tm, LRU_WIDTH), lambda i: (i, 0))
    outs = pl.pallas_call(
        _inproj_kernel,
        out_shape=att_shapes * 3 + [jax.ShapeDtypeStruct((t, LRU_WIDTH), F32)] * 2,
        grid=(t // tm,),
        in_specs=[
            pl.BlockSpec((tm, d), lambda i: (i, 0)),
            wspec(ATT_WIDTH, 0), wspec(ATT_WIDTH, 1), wspec(ATT_WIDTH, 2),
            wspec(LRU_WIDTH, 0), wspec(LRU_WIDTH, 1),
            tab_spec, tab_spec, tab_spec,
        ],
        out_specs=att_specs * 3 + [lru_spec, lru_spec],
        scratch_shapes=[pltpu.VMEM((HEAD_PAIRS, tm, LANES), F32)],
        compiler_params=_params("parallel"),
        name="inproj",
    )(x, w_qkv, w_qkv, w_qkv, w_lru, w_lru, *tables)
    n_pat = len(DILATIONS)
    return outs[:n_pat], outs[n_pat:2 * n_pat], outs[2 * n_pat:3 * n_pat], outs[-2], outs[-1]


def _attn_kernel(q_ref, km_ref, kl_ref, kr_ref, vm_ref, vl_ref, vr_ref, o_ref, lse_ref,
                 kc_ref, vc_ref, *, tq, geom):
    rows_p, len_p, len_s = geom
    row0 = pl.program_id(0) * tq
    lo, hi = _seq_bounds(row0, rows_p, len_p, len_s)

    kc_ref[0:RADIUS] = kl_ref[...]
    kc_ref[RADIUS:RADIUS + tq] = km_ref[...]
    kc_ref[RADIUS + tq:] = kr_ref[...]
    vc_ref[0:RADIUS] = vl_ref[...]
    vc_ref[RADIUS:RADIUS + tq] = vm_ref[...]
    vc_ref[RADIUS + tq:] = vr_ref[...]

    qb_rows = LANES
    kb_rows = qb_rows + 2 * RADIUS
    lane = lax.broadcasted_iota(jnp.int32, (qb_rows, LANES), 1)
    low_half = lane < HEAD_DIM
    trow = lax.broadcasted_iota(jnp.int32, (qb_rows, kb_rows), 0)
    ccol = lax.broadcasted_iota(jnp.int32, (qb_rows, kb_rows), 1)
    band = (ccol >= trow) & (ccol <= trow + 2 * RADIUS)

    for qb in range(tq // qb_rows):
        r0 = qb * qb_rows
        base = row0 + r0 - RADIUS
        valid = band & (ccol >= lo - base) & (ccol < hi - base)
        valid2 = jnp.concatenate([valid, valid], axis=0)
        m_tile = jnp.zeros((qb_rows, LANES), F32)
        l_tile = jnp.ones((qb_rows, LANES), F32)
        for p in range(HEAD_PAIRS):
            cols = slice(p * LANES, (p + 1) * LANES)
            qp = q_ref[r0:r0 + qb_rows, cols]
            kp = kc_ref[r0:r0 + kb_rows, cols]
            vp = vc_ref[r0:r0 + kb_rows, cols]
            zero = jnp.zeros_like(qp)
            q2 = jnp.concatenate([jnp.where(low_half, qp, zero), jnp.where(low_half, zero, qp)], axis=0)
            s = lax.dot_general(q2, kp, (((1,), (1,)), ((), ())), preferred_element_type=F32)
            s = jnp.where(valid2, s, NEG_INF)
            m = jnp.max(s, axis=1, keepdims=True)
            e = jnp.exp2(s - m)
            l = jnp.sum(e, axis=1, keepdims=True)
            pv = jnp.dot(e.astype(BF16), vp, preferred_element_type=F32) / l
            o_ref[r0:r0 + qb_rows, cols] = jnp.where(low_half, pv[:qb_rows], pv[qb_rows:])
            for hh, rows in enumerate((slice(0, qb_rows), slice(qb_rows, 2 * qb_rows))):
                m_tile = jnp.where(lane == 2 * p + hh, m[rows], m_tile)
                l_tile = jnp.where(lane == 2 * p + hh, l[rows], l_tile)
        lse_ref[r0:r0 + qb_rows, :] = m_tile * LN2 + jnp.log(l_tile)


def _attn_pattern(q, k, v, dil, geom, max_tq=512):
    rows = q.shape[0]
    rows_p, len_p, len_s = (g // dil for g in geom)
    tq = min(max_tq, len_p, len_s)
    assert rows % tq == 0 and len_p % tq == 0 and len_s % tq == 0 and tq % RADIUS == 0
    hb = tq // RADIUS
    n_halo = rows // RADIUS
    main = pl.BlockSpec((tq, ATT_WIDTH), lambda i, r: (i, r))
    left = pl.BlockSpec((RADIUS, ATT_WIDTH), lambda i, r: (jnp.maximum(i * hb - 1, 0), r))
    right = pl.BlockSpec((RADIUS, ATT_WIDTH), lambda i, r: (jnp.minimum((i + 1) * hb, n_halo - 1), r))
    return pl.pallas_call(
        functools.partial(_attn_kernel, tq=tq, geom=(rows_p, len_p, len_s)),
        out_shape=[jax.ShapeDtypeStruct((rows, dil * ATT_WIDTH), F32),
                   jax.ShapeDtypeStruct((rows, dil * LANES), F32)],
        grid=(rows // tq, dil),
        in_specs=[main, main, left, right, main, left, right],
        out_specs=[pl.BlockSpec((tq, ATT_WIDTH), lambda i, r: (i, r)),
                   pl.BlockSpec((tq, LANES), lambda i, r: (i, r))],
        scratch_shapes=[pltpu.VMEM((tq + 2 * RADIUS, ATT_WIDTH), BF16),
                        pltpu.VMEM((tq + 2 * RADIUS, ATT_WIDTH), BF16)],
        compiler_params=_params("parallel", "parallel"),
        name=f"attn_d{dil}",
    )(q, k, k, k, v, v, v)


def _sublane_scan(h_end, p_end, carry, sub, reverse):
    n = SUBLANES
    k = 1
    while k < n:
        if reverse:
            keep, shift = sub < n - k, n - k
        else:
            keep, shift = sub >= k, k
        h_sh = jnp.where(keep, pltpu.roll(h_end, shift, 0), 0.0)
        p_sh = jnp.where(keep, pltpu.roll(p_end, shift, 0), 1.0)
        h_end = p_end * h_sh + h_end
        p_end = p_end * p_sh
        k *= 2
    leave = h_end + p_end * carry
    if reverse:
        enter = jnp.where(sub < n - 1, pltpu.roll(leave, n - 1, 0), carry)
        return enter, leave[0:1]
    enter = jnp.where(sub >= 1, pltpu.roll(leave, 1, 0), carry)
    return enter, leave[n - 1:n]


def _lru_kernel(xm_ref, xp_ref, xn_ref, cw_ref, cb_ref, w_ref, ba_ref, bx_ref, lam_ref, h_ref,
                carry_ref, xs_ref, hs_ref, *, tc, nc, reverse, geom):
    rows_p, len_p, len_s = geom
    c = pl.program_id(0)
    row0 = ((nc - 1 - c) if reverse else c) * tc
    lo, hi = _seq_bounds(row0, rows_p, len_p, len_s)
    at_start = row0 == lo
    at_end = row0 + tc == hi

    @pl.when(at_end if reverse else at_start)
    def _():
        carry_ref[...] = jnp.zeros_like(carry_ref)

    halo = xp_ref.shape[0]
    steps = tc // SUBLANES
    pitch = steps + SCAN_PITCH_PAD
    conv_right = CONV_WIDTH - 1 - CONV_LEFT
    assert CONV_WIDTH - 1 <= SCAN_PITCH_PAD and CONV_LEFT <= halo and conv_right <= halo
    for n in range(LRU_BLOCKS):
        cols = slice(n * LRU_BLOCK, (n + 1) * LRU_BLOCK)
        for j in range(SUBLANES):
            r0 = steps * j
            xs_ref[n, pitch * j + CONV_LEFT:pitch * j + CONV_LEFT + steps, :] = xm_ref[r0:r0 + steps, cols]
            if j == 0:
                left = jnp.where(at_start, 0.0, xp_ref[halo - CONV_LEFT:halo, cols])
            else:
                left = xm_ref[r0 - CONV_LEFT:r0, cols]
            xs_ref[n, pitch * j:pitch * j + CONV_LEFT, :] = left
            if j == SUBLANES - 1:
                right = jnp.where(at_end, 0.0, xn_ref[0:conv_right, cols])
            else:
                right = xm_ref[r0 + steps:r0 + steps + conv_right, cols]
            r1 = pitch * j + CONV_LEFT + steps
            xs_ref[n, r1:r1 + conv_right, :] = right

    cw = cw_ref[...]
    cb = cb_ref[...]
    sub = lax.broadcasted_iota(jnp.int32, (SUBLANES, LRU_BLOCK), 0)
    neg = -lam_ref[...]
    softplus = jnp.maximum(neg, 0.0) + jnp.log1p(jnp.exp(-jnp.abs(neg)))
    cl = -LRU_C * softplus
    order = range(steps - 1, -1, -1) if reverse else range(steps)
    for n in range(LRU_BLOCKS):
        cols = slice(n * LRU_BLOCK, (n + 1) * LRU_BLOCK)
        taps = [jnp.concatenate([xs_ref[n, pl.ds(s + tap, SUBLANES, stride=pitch), :] for s in range(steps)],
                                axis=0) * cw[tap:tap + 1, cols] for tap in range(CONV_WIDTH)]
        xq = functools.reduce(lambda x, y: x + y, taps) + cb[:, cols]
        gates = jnp.dot(xq.astype(BF16), w_ref[n], preferred_element_type=F32)
        r = _sigmoid(gates[:, :LRU_BLOCK] + ba_ref[:, cols])
        i = _sigmoid(gates[:, LRU_BLOCK:] + bx_ref[:, cols])
        a = jnp.exp(cl[:, cols] * r)
        u = jnp.sqrt(1.0 - a * a) * i * xq

        h = jnp.zeros((SUBLANES, LRU_BLOCK), F32)
        pr = jnp.ones((SUBLANES, LRU_BLOCK), F32)
        h_loc, p_loc = {}, {}
        for s in order:
            a_s = a[SUBLANES * s:SUBLANES * (s + 1)]
            h = a_s * h + u[SUBLANES * s:SUBLANES * (s + 1)]
            pr = a_s * pr
            h_loc[s], p_loc[s] = h, pr
        enter, leave = _sublane_scan(h, pr, carry_ref[:, cols], sub, reverse)
        carry_ref[:, cols] = leave
        for s in range(steps):
            hs_ref[n, pl.ds(s, SUBLANES, stride=pitch), :] = h_loc[s] + p_loc[s] * enter

    for j in range(SUBLANES):
        for n in range(LRU_BLOCKS):
            h_ref[steps * j:steps * (j + 1), n * LRU_BLOCK:(n + 1) * LRU_BLOCK] = (
                hs_ref[n, pitch * j:pitch * j + steps, :])


def _lru_direction(xl, cw, cb, w_gate, ba, bx, lam, layer, direction, geom, tc=512, halo=8):
    t = xl.shape[0]
    nc = t // tc
    rows_p, len_p, len_s = geom
    assert t % tc == 0 and len_p % tc == 0 and len_s % tc == 0 and tc % halo == 0
    reverse = direction == 1
    hb = tc // halo
    n_halo = t // halo
    stage_rows = SUBLANES * (tc // SUBLANES + SCAN_PITCH_PAD)

    def blk(c):
        return (nc - 1 - c) if reverse else c

    vec = pl.BlockSpec((None, None, 1, LRU_WIDTH), lambda c: (layer, direction, 0, 0))
    return pl.pallas_call(
        functools.partial(_lru_kernel, tc=tc, nc=nc, reverse=reverse, geom=geom),
        out_shape=jax.ShapeDtypeStruct((t, LRU_WIDTH), F32),
        grid=(nc,),
        in_specs=[
            pl.BlockSpec((tc, LRU_WIDTH), lambda c: (blk(c), 0)),
            pl.BlockSpec((halo, LRU_WIDTH), lambda c: (jnp.maximum(blk(c) * hb - 1, 0), 0)),
            pl.BlockSpec((halo, LRU_WIDTH), lambda c: (jnp.minimum((blk(c) + 1) * hb, n_halo - 1), 0)),
            pl.BlockSpec((None, CONV_WIDTH, LRU_WIDTH), lambda c: (layer, 0, 0)),
            pl.BlockSpec((None, 1, LRU_WIDTH), lambda c: (layer, 0, 0)),
            pl.BlockSpec((None, None, LRU_BLOCKS, LRU_BLOCK, 2 * LRU_BLOCK),
                         lambda c: (layer, direction, 0, 0, 0)),
            vec, vec, vec,
        ],
        out_specs=pl.BlockSpec((tc, LRU_WIDTH), lambda c: (blk(c), 0)),
        scratch_shapes=[pltpu.VMEM((1, LRU_WIDTH), F32),
                        pltpu.VMEM((LRU_BLOCKS, stage_rows, LRU_BLOCK), F32),
                        pltpu.VMEM((LRU_BLOCKS, stage_rows, LRU_BLOCK), F32)],
        compiler_params=_params("arbitrary"),
        name="lru_bwd" if reverse else "lru_fwd",
    )(xl, xl, xl, cw, cb, w_gate, ba, bx, lam)


def _merge_kernel(*refs, alpha):
    n_pat = len(DILATIONS)
    x_ref = refs[0]
    o_refs = refs[1:1 + n_pat]
    l_refs = refs[1 + n_pat:1 + 2 * n_pat]
    (hf_ref, hb_ref, gl_ref, ag_ref, lg_ref, wo_ref, g_ref, b_ref, out_ref,
     m0_ref, m1_ref, os_ref, ls_ref) = refs[1 + 2 * n_pat:]
    tm = x_ref.shape[0]
    i = pl.program_id(0)

    def normalise(m_ref):
        for g, dil in enumerate(DILATIONS):
            if dil == 1:
                continue
            for r in range(dil):
                ls_ref[g, pl.ds(r, tm // dil, stride=dil), :] = l_refs[g][:, r * LANES:(r + 1) * LANES]
                for p in range(HEAD_PAIRS):
                    c0 = r * ATT_WIDTH + p * LANES
                    os_ref[g, p, pl.ds(r, tm // dil, stride=dil), :] = o_refs[g][:, c0:c0 + LANES]

        lses = [l_refs[g][...] if dil == 1 else ls_ref[g] for g, dil in enumerate(DILATIONS)]
        mx = functools.reduce(jnp.maximum, lses)
        es = [jnp.exp(l - mx) for l in lses]
        den = functools.reduce(lambda a, b: a + b, es)
        ws = [e / den for e in es]
        low_half = lax.broadcasted_iota(jnp.int32, (tm, LANES), 1) < HEAD_DIM
        pieces = []
        for p in range(HEAD_PAIRS):
            acc = None
            for g, (dil, w) in enumerate(zip(DILATIONS, ws)):
                o_gp = o_refs[g][:, p * LANES:(p + 1) * LANES] if dil == 1 else os_ref[g, p]
                wsel = jnp.where(low_half, w[:, 2 * p:2 * p + 1], w[:, 2 * p + 1:2 * p + 2])
                term = o_gp * wsel
                acc = term if acc is None else acc + term
            pieces.append(acc)
        attn = jnp.concatenate(pieces, axis=1)
        m_ref[:, :ATT_WIDTH] = _rms_norm(attn, ag_ref[...]).astype(BF16)
        rec = (hf_ref[...] + hb_ref[...]) * jax.nn.gelu(gl_ref[...])
        m_ref[:, ATT_WIDTH:] = _rms_norm(rec, lg_ref[...]).astype(BF16)

    def project(m_ref):
        mix = jnp.dot(m_ref[...], wo_ref[...], preferred_element_type=F32)
        out_ref[...] = _layer_norm(alpha * x_ref[...] + mix, g_ref[...], b_ref[...])

    @pl.when(i == 0)
    def _():
        m1_ref[...] = jnp.zeros_like(m1_ref)

    parity = lax.rem(i, 2)

    @pl.when(parity == 0)
    def _():
        normalise(m0_ref)
        project(m1_ref)

    @pl.when(parity == 1)
    def _():
        normalise(m1_ref)
        project(m0_ref)


def _merge(x, os_, lses, hf, hb, gl, ag, lg, wo, g, b, layer, alpha, tm=256):
    t, d = x.shape
    assert t % tm == 0
    nt = t // tm
    n_pat = len(DILATIONS)

    def fresh(width, dil=1):
        return pl.BlockSpec((tm // dil, dil * width), lambda i: (jnp.minimum(i, nt - 1), 0))

    lagged = pl.BlockSpec((tm, d), lambda i: (jnp.maximum(i - 1, 0), 0))

    def vec(width):
        return pl.BlockSpec((None, 1, width), lambda i: (layer, 0, 0))

    return pl.pallas_call(
        functools.partial(_merge_kernel, alpha=alpha),
        out_shape=jax.ShapeDtypeStruct((t, d), F32),
        grid=(nt + 1,),
        in_specs=[lagged] + [fresh(ATT_WIDTH, dil) for dil in DILATIONS]
        + [fresh(LANES, dil) for dil in DILATIONS] + [fresh(LRU_WIDTH)] * 3
        + [vec(ATT_WIDTH), vec(LRU_WIDTH),
           pl.BlockSpec((None, d, d), lambda i: (layer, 0, 0), pipeline_mode=pl.Buffered(1)),
           vec(d), vec(d)],
        out_specs=lagged,
        scratch_shapes=[pltpu.VMEM((tm, d), BF16), pltpu.VMEM((tm, d), BF16),
                        pltpu.VMEM((n_pat, HEAD_PAIRS, tm, LANES), F32),
                        pltpu.VMEM((n_pat, tm, LANES), F32)],
        compiler_params=_params("arbitrary"),
        name="merge",
    )(x, *os_, *lses, hf, hb, gl, ag, lg, wo, g, b)


def kernel(x_prompt, x_sample, ln1_g, ln1_b, ffn1_w_gate, ffn1_w_up, ffn1_w_down, ln2_g, ln2_b, w_in, conv_w, conv_b, lru_w_a, lru_b_a, lru_w_x, lru_b_x, lru_lambda, att_norm_g, lru_norm_g, w_out, ln3_g, ln3_b, ffn2_w_gate, ffn2_w_up, ffn2_w_down):
    nb, s, d = x_prompt.shape
    db, ds, _ = x_sample.shape
    depth = ln1_g.shape[0]
    alpha = (2 * depth) ** 0.25
    assert d == ATT_WIDTH + LRU_WIDTH
    geom = (nb * s, s, ds)

    tables = _rope_tables(max(s, ds))
    row3 = lambda a: a.reshape(depth, 1, a.shape[-1])
    ln1 = (row3(ln1_g), row3(ln1_b))
    ln2 = (row3(ln2_g), row3(ln2_b))
    ln3 = (row3(ln3_g), row3(ln3_b))
    ffn1 = tuple(w.astype(BF16) for w in (ffn1_w_gate, ffn1_w_up, ffn1_w_down))
    ffn2 = tuple(w.astype(BF16) for w in (ffn2_w_gate, ffn2_w_up, ffn2_w_down))
    w_in_b = w_in.astype(BF16)
    w_qkv, w_lru = w_in_b[:, :, :3 * ATT_WIDTH], w_in_b[:, :, 3 * ATT_WIDTH:]
    w_gate = jnp.concatenate([lru_w_a, lru_w_x], axis=-1).astype(BF16)
    vec4 = lambda a: a.reshape(depth, 2, 1, LRU_WIDTH)
    ba, bx, lam = vec4(lru_b_a), vec4(lru_b_x), vec4(lru_lambda)
    cb = row3(conv_b)
    ag, lg = row3(att_norm_g), row3(lru_norm_g)
    w_out_b = w_out.astype(BF16)

    xs = (x_prompt.reshape(nb * s, d), x_sample.reshape(db * ds, d))
    for layer in range(depth):
        (x,) = _ffn(xs, *ffn1, *ln1, layer, alpha, nb * s)
        qs, ks, vs, xl, gl = _inproj(x, w_qkv, w_lru, tables, layer, geom)
        os_, lses = [], []
        for g, dil in enumerate(DILATIONS):
            o, lse = _attn_pattern(qs[g], ks[g], vs[g], dil, geom)
            os_.append(o)
            lses.append(lse)
        hf = _lru_direction(xl, conv_w, cb, w_gate, ba, bx, lam, layer, 0, geom)
        hb = _lru_direction(xl, conv_w, cb, w_gate, ba, bx, lam, layer, 1, geom)
        x = _merge(x, os_, lses, hf, hb, gl, ag, lg, w_out_b, *ln2, layer, alpha)
        xs = _ffn((x,), *ffn2, *ln3, layer, alpha, nb * s, split_out=layer == depth - 1)

    y_prompt, y_sample = xs
    return (y_prompt.reshape(nb, s, d), y_sample.reshape(db, ds, d))
```

```python
import functools

import jax
import jax.numpy as jnp
from jax import lax
from jax.experimental import pallas as pl
from jax.experimental.pallas import tpu as pltpu

HEAD_DIM = 64
ATT_HEADS = 12
ATT_WIDTH = ATT_HEADS * HEAD_DIM
LRU_BLOCKS = 10
LRU_BLOCK = 128
LRU_WIDTH = LRU_BLOCKS * LRU_BLOCK
CONV_WIDTH = 4
CONV_LEFT = 2
LRU_C = 8.0
ROPE_THETA = 500000.0
ROT_DIM = HEAD_DIM // 4
DILATED_PATTERNS = ((128, 1), (512, 4), (2048, 16))
DILATIONS = tuple(d for _, d in DILATED_PATTERNS)
RADIUS = 64
assert all(w // (2 * d) == RADIUS for w, d in DILATED_PATTERNS)
NORM_EPS = 1e-5
NEG_INF = -1e30
F32_MIN_NORMAL = 1.1754943508222875e-38
LN2 = 0.6931471805599453
LOG2E = 1.4426950408889634

LANES = 128
SUBLANES = 8
HEAD_PAIRS = ATT_WIDTH // LANES
VMEM_LIMIT_BYTES = 56 * 1024 * 1024
FFN_TF = 512
SCAN_PITCH_PAD = 4

F32 = jnp.float32
BF16 = jnp.bfloat16


def _params(*semantics):
    return pltpu.CompilerParams(dimension_semantics=semantics, vmem_limit_bytes=VMEM_LIMIT_BYTES)


def _layer_norm(y, g, b):
    mu = jnp.mean(y, axis=-1, keepdims=True)
    yc = y - mu
    var = jnp.mean(yc * yc, axis=-1, keepdims=True)
    return yc * lax.rsqrt(var + NORM_EPS) * g + b


def _rms_norm(y, g):
    ms = jnp.mean(y * y, axis=-1, keepdims=True)
    return y * lax.rsqrt(ms + NORM_EPS) * g


def _seq_bounds(row0, rows_p, len_p, len_s):
    in_p = row0 < rows_p
    lo_p = lax.div(row0, len_p) * len_p
    lo_s = rows_p + lax.div(jnp.maximum(row0 - rows_p, 0), len_s) * len_s
    lo = jnp.where(in_p, lo_p, lo_s)
    hi = jnp.where(in_p, lo_p + len_p, lo_s + len_s)
    return lo, hi


def _ffn_kernel(*refs, alpha, nj, n_p, split_in, split_out):
    n_x = 2 if split_in else 1
    n_o = 2 if split_out else 1
    x_refs = refs[:n_x]
    wg_ref, wu_ref, wd_ref, g_ref, b_ref = refs[n_x:n_x + 5]
    o_refs = refs[n_x + 5:n_x + 5 + n_o]
    xb_ref, acc_ref = refs[n_x + 5 + n_o:]
    i, j = pl.program_id(0), pl.program_id(1)
    groups = (i < n_p, i >= n_p)

    def start(x_ref):
        xb_ref[...] = x_ref[...].astype(BF16)
        acc_ref[...] = jnp.zeros_like(acc_ref)

    def finish(x_ref, o_ref):
        y = alpha * x_ref[...] + 0.5 * acc_ref[...]
        o_ref[...] = _layer_norm(y, g_ref[...], b_ref[...])

    if split_in:
        for x_ref, grp in zip(x_refs, groups):
            pl.when((j == 0) & grp)(functools.partial(start, x_ref))
    else:
        pl.when(j == 0)(functools.partial(start, x_refs[0]))

    xb = xb_ref[...]
    gate = jnp.dot(xb, wg_ref[...], preferred_element_type=F32)
    up = jnp.dot(xb, wu_ref[...], preferred_element_type=F32)
    h = (gate * jax.nn.sigmoid(gate)) * up
    acc_ref[...] += jnp.dot(h.astype(BF16), wd_ref[...], preferred_element_type=F32)

    last = j == nj - 1
    if split_in or split_out:
        for g_idx, grp in enumerate(groups):
            x_ref = x_refs[g_idx] if split_in else x_refs[0]
            o_ref = o_refs[g_idx] if split_out else o_refs[0]
            pl.when(last & grp)(functools.partial(finish, x_ref, o_ref))
    else:
        pl.when(last)(functools.partial(finish, x_refs[0], o_refs[0]))


def _ffn_column_tiles(w, tf=FFN_TF):
    depth, d, f = w.shape
    assert f % tf == 0
    return w.astype(BF16).reshape(depth, d, f // tf, tf).transpose(0, 2, 1, 3)


def _ffn(xs, wg, wu, wd, g, b, layer, alpha, n_p_rows, split_out=False, tm=512):
    split_in = len(xs) == 2
    d = xs[0].shape[1]
    t = sum(x.shape[0] for x in xs)
    nj, tf = wg.shape[1], wg.shape[3]
    assert t % tm == 0 and n_p_rows % tm == 0 and wd.shape[1] == nj * tf
    n_p = n_p_rows // tm

    def grouped_specs():
        return [pl.BlockSpec((tm, d), lambda i, j: (jnp.minimum(i, n_p - 1), 0)),
                pl.BlockSpec((tm, d), lambda i, j: (jnp.maximum(i - n_p, 0), 0))]

    flat_spec = [pl.BlockSpec((tm, d), lambda i, j: (i, 0))]
    if split_out:
        out_shape = [jax.ShapeDtypeStruct((n_p_rows, d), F32), jax.ShapeDtypeStruct((t - n_p_rows, d), F32)]
    else:
        out_shape = [jax.ShapeDtypeStruct((t, d), F32)]
    return pl.pallas_call(
        functools.partial(_ffn_kernel, alpha=alpha, nj=nj, n_p=n_p, split_in=split_in, split_out=split_out),
        out_shape=out_shape,
        grid=(t // tm, nj),
        in_specs=(grouped_specs() if split_in else flat_spec) + [
            pl.BlockSpec((None, None, d, tf), lambda i, j: (layer, j, 0, 0)),
            pl.BlockSpec((None, None, d, tf), lambda i, j: (layer, j, 0, 0)),
            pl.BlockSpec((None, tf, d), lambda i, j: (layer, j, 0)),
            pl.BlockSpec((None, 1, d), lambda i, j: (layer, 0, 0)),
            pl.BlockSpec((None, 1, d), lambda i, j: (layer, 0, 0)),
        ],
        out_specs=grouped_specs() if split_out else flat_spec,
        scratch_shapes=[pltpu.VMEM((tm, d), BF16), pltpu.VMEM((tm, d), F32)],
        compiler_params=_params("arbitrary", "arbitrary"),
        name="ffn",
    )(*xs, wg, wu, wd, g, b)


def _rotary(t, cos_t, sin_lo, sin_hi):
    half = ROT_DIM // 2
    out = []
    for p in range(HEAD_PAIRS):
        tp = t[:, p * LANES:(p + 1) * LANES]
        up = pltpu.roll(tp, LANES - half, 1)
        dn = pltpu.roll(tp, half, 1)
        out.append(tp * cos_t + up * sin_lo + dn * sin_hi)
    return jnp.concatenate(out, axis=1)


def _emit_dilated(t, out_refs, stage_ref):
    rows = t.shape[0]
    for dil, out_ref in zip(DILATIONS, out_refs):
        if dil == 1:
            out_ref[...] = t.astype(BF16)
    for p in range(HEAD_PAIRS):
        stage_ref[p] = t[:, p * LANES:(p + 1) * LANES]
    for dil, out_ref in zip(DILATIONS, out_refs):
        if dil == 1:
            continue
        for r in range(dil):
            for p in range(HEAD_PAIRS):
                piece = stage_ref[p, pl.ds(r, rows // dil, stride=dil), :]
                c0 = r * ATT_WIDTH + p * LANES
                out_ref[:, c0:c0 + LANES] = piece.astype(BF16)


def _inproj_kernel(x_ref, wq_ref, wk_ref, wv_ref, wx_ref, wg_ref, cos_ref, slo_ref, shi_ref, *rest):
    n_pat = len(DILATIONS)
    q_refs, k_refs, v_refs = rest[:n_pat], rest[n_pat:2 * n_pat], rest[2 * n_pat:3 * n_pat]
    xl_ref, gl_ref, stage_ref = rest[3 * n_pat:]
    xb = x_ref[...].astype(BF16)
    cos_t, sin_lo, sin_hi = cos_ref[...], slo_ref[...], shi_ref[...]
    q = jnp.dot(xb, wq_ref[...], preferred_element_type=F32)
    _emit_dilated(_rotary(q, cos_t, sin_lo, sin_hi) * (HEAD_DIM ** -0.5 * LOG2E), q_refs, stage_ref)
    k = jnp.dot(xb, wk_ref[...], preferred_element_type=F32)
    _emit_dilated(_rotary(k, cos_t, sin_lo, sin_hi), k_refs, stage_ref)
    _emit_dilated(jnp.dot(xb, wv_ref[...], preferred_element_type=F32), v_refs, stage_ref)
    xl_ref[...] = jnp.dot(xb, wx_ref[...], preferred_element_type=F32)
    gl_ref[...] = jnp.dot(xb, wg_ref[...], preferred_element_type=F32)


def _rope_tables(n_pos):
    half = ROT_DIM // 2
    inv_freq = ROPE_THETA ** (-jnp.arange(0, ROT_DIM, 2, dtype=F32) / ROT_DIM)
    ang = jnp.arange(n_pos, dtype=F32)[:, None] * inv_freq[None, :]
    cos, sin = jnp.cos(ang), jnp.sin(ang)
    rest = HEAD_DIM - ROT_DIM
    one = jnp.ones((n_pos, rest), F32)
    zero = jnp.zeros((n_pos, rest), F32)
    zh = jnp.zeros((n_pos, half), F32)
    reps = LANES // HEAD_DIM
    cos_t = jnp.tile(jnp.concatenate([cos, cos, one], axis=1), (1, reps))
    sin_lo = jnp.tile(jnp.concatenate([-sin, zh, zero], axis=1), (1, reps))
    sin_hi = jnp.tile(jnp.concatenate([zh, sin, zero], axis=1), (1, reps))
    return cos_t, sin_lo, sin_hi


def _inproj(x, w_qkv, w_lru, tables, layer, geom, tm=256):
    t, d = x.shape
    rows_p, len_p, len_s = geom
    assert t % tm == 0 and len_p % tm == 0 and len_s % tm == 0
    n_p, bp, bs = rows_p // tm, len_p // tm, len_s // tm

    def pos_map(i):
        return (jnp.where(i < n_p, lax.rem(i, bp), lax.rem(jnp.maximum(i - n_p, 0), bs)), 0)

    def wspec(width, col):
        return pl.BlockSpec((None, d, width), lambda i: (layer, 0, col), pipeline_mode=pl.Buffered(1))

    tab_spec = pl.BlockSpec((tm, LANES), pos_map)
    att_shapes = [jax.ShapeDtypeStruct((t // dil, dil * ATT_WIDTH), BF16) for dil in DILATIONS]
    att_specs = [pl.BlockSpec((tm // dil, dil * ATT_WIDTH), lambda i: (i, 0)) for dil in DILATIONS]
    lru_spec = pl.BlockSpec((tm, LRU_WIDTH), lambda i: (i, 0))
    outs = pl.pallas_call(
        _inproj_kernel,
        out_shape=att_shapes * 3 + [jax.ShapeDtypeStruct((t, LRU_WIDTH), F32)] * 2,
        grid=(t // tm,),
        in_specs=[
            pl.BlockSpec((tm, d), lambda i: (i, 0)),
            wspec(ATT_WIDTH, 0), wspec(ATT_WIDTH, 1), wspec(ATT_WIDTH, 2),
            wspec(LRU_WIDTH, 0), wspec(LRU_WIDTH, 1),
            tab_spec, tab_spec, tab_spec,
        ],
        out_specs=att_specs * 3 + [lru_spec, lru_spec],
        scratch_shapes=[pltpu.VMEM((HEAD_PAIRS, tm, LANES), F32)],
        compiler_params=_params("parallel"),
        name="inproj",
    )(x, w_qkv, w_qkv, w_qkv, w_lru, w_lru, *tables)
    n_pat = len(DILATIONS)
    return outs[:n_pat], outs[n_pat:2 * n_pat], outs[2 * n_pat:3 * n_pat], outs[-2], outs[-1]


def _attn_kernel(q_ref, km_ref, kl_ref, kr_ref, vm_ref, vl_ref, vr_ref, o_ref, lse_ref,
                 kc_ref, vc_ref, *, tq, geom):
    rows_p, len_p, len_s = geom
    row0 = pl.program_id(0) * tq
    lo, hi = _seq_bounds(row0, rows_p, len_p, len_s)

    kc_ref[0:RADIUS] = kl_ref[...]
    kc_ref[RADIUS:RADIUS + tq] = km_ref[...]
    kc_ref[RADIUS + tq:] = kr_ref[...]
    vc_ref[0:RADIUS] = vl_ref[...]
    vc_ref[RADIUS:RADIUS + tq] = vm_ref[...]
    vc_ref[RADIUS + tq:] = vr_ref[...]

    qb_rows = LANES
    kb_rows = qb_rows + 2 * RADIUS
    lane = lax.broadcasted_iota(jnp.int32, (qb_rows, LANES), 1)
    low_half = lane < HEAD_DIM
    trow = lax.broadcasted_iota(jnp.int32, (qb_rows, kb_rows), 0)
    ccol = lax.broadcasted_iota(jnp.int32, (qb_rows, kb_rows), 1)
    band = (ccol >= trow) & (ccol <= trow + 2 * RADIUS)

    for qb in range(tq // qb_rows):
        r0 = qb * qb_rows
        base = row0 + r0 - RADIUS
        valid = band & (ccol >= lo - base) & (ccol < hi - base)
        valid2 = jnp.concatenate([valid, valid], axis=0)
        m_tile = jnp.zeros((qb_rows, LANES), F32)
        l_tile = jnp.ones((qb_rows, LANES), F32)
        for p in range(HEAD_PAIRS):
            cols = slice(p * LANES, (p + 1) * LANES)
            qp = q_ref[r0:r0 + qb_rows, cols]
            kp = kc_ref[r0:r0 + kb_rows, cols]
            vp = vc_ref[r0:r0 + kb_rows, cols]
            zero = jnp.zeros_like(qp)
            q2 = jnp.concatenate([jnp.where(low_half, qp, zero), jnp.where(low_half, zero, qp)], axis=0)
            s = lax.dot_general(q2, kp, (((1,), (1,)), ((), ())), preferred_element_type=F32)
            s = jnp.where(valid2, s, NEG_INF)
            m = jnp.max(s, axis=1, keepdims=True)
            e = jnp.exp2(s - m)
            l = jnp.sum(e, axis=1, keepdims=True)
            pv = jnp.dot(e.astype(BF16), vp, preferred_element_type=F32) / l
            o_ref[r0:r0 + qb_rows, cols] = jnp.where(low_half, pv[:qb_rows], pv[qb_rows:])
            for hh, rows in enumerate((slice(0, qb_rows), slice(qb_rows, 2 * qb_rows))):
                m_tile = jnp.where(lane == 2 * p + hh, m[rows], m_tile)
                l_tile = jnp.where(lane == 2 * p + hh, l[rows], l_tile)
        lse_ref[r0:r0 + qb_rows, :] = m_tile * LN2 + jnp.log(l_tile)


def _attn_pattern(q, k, v, dil, geom, max_tq=512):
    rows = q.shape[0]
    rows_p, len_p, len_s = (g // dil for g in geom)
    tq = min(max_tq, len_p, len_s)
    assert rows % tq == 0 and len_p % tq == 0 and len_s % tq == 0 and tq % RADIUS == 0
    hb = tq // RADIUS
    n_halo = rows // RADIUS
    main = pl.BlockSpec((tq, ATT_WIDTH), lambda i, r: (i, r))
    left = pl.BlockSpec((RADIUS, ATT_WIDTH), lambda i, r: (jnp.maximum(i * hb - 1, 0), r))
    right = pl.BlockSpec((RADIUS, ATT_WIDTH), lambda i, r: (jnp.minimum((i + 1) * hb, n_halo - 1), r))
    return pl.pallas_call(
        functools.partial(_attn_kernel, tq=tq, geom=(rows_p, len_p, len_s)),
        out_shape=[jax.ShapeDtypeStruct((rows, dil * ATT_WIDTH), F32),
                   jax.ShapeDtypeStruct((rows, dil * LANES), F32)],
        grid=(rows // tq, dil),
        in_specs=[main, main, left, right, main, left, right],
        out_specs=[pl.BlockSpec((tq, ATT_WIDTH), lambda i, r: (i, r)),
                   pl.BlockSpec((tq, LANES), lambda i, r: (i, r))],
        scratch_shapes=[pltpu.VMEM((tq + 2 * RADIUS, ATT_WIDTH), BF16),
                        pltpu.VMEM((tq + 2 * RADIUS, ATT_WIDTH), BF16)],
        compiler_params=_params("parallel", "parallel"),
        name=f"attn_d{dil}",
    )(q, k, k, k, v, v, v)


def _sublane_scan(h_end, p_end, carry, sub, reverse):
    n = SUBLANES
    k = 1
    while k < n:
        if reverse:
            keep, shift = sub < n - k, n - k
        else:
            keep, shift = sub >= k, k
        h_sh = jnp.where(keep, pltpu.roll(h_end, shift, 0), 0.0)
        p_sh = jnp.where(keep, pltpu.roll(p_end, shift, 0), 1.0)
        h_end = p_end * h_sh + h_end
        p_end = p_end * p_sh
        k *= 2
    leave = h_end + p_end * carry
    if reverse:
        enter = jnp.where(sub < n - 1, pltpu.roll(leave, n - 1, 0), carry)
        return enter, leave[0:1]
    enter = jnp.where(sub >= 1, pltpu.roll(leave, 1, 0), carry)
    return enter, leave[n - 1:n]


def _lru_kernel(xm_ref, xp_ref, xn_ref, cw_ref, cb_ref, w_ref, ba_ref, bx_ref, lam_ref, h_ref,
                carry_ref, xs_ref, hs_ref, *, tc, nc, reverse, geom):
    rows_p, len_p, len_s = geom
    c = pl.program_id(0)
    row0 = ((nc - 1 - c) if reverse else c) * tc
    lo, hi = _seq_bounds(row0, rows_p, len_p, len_s)
    at_start = row0 == lo
    at_end = row0 + tc == hi

    @pl.when(at_end if reverse else at_start)
    def _():
        carry_ref[...] = jnp.zeros_like(carry_ref)

    halo = xp_ref.shape[0]
    steps = tc // SUBLANES
    pitch = steps + SCAN_PITCH_PAD
    conv_right = CONV_WIDTH - 1 - CONV_LEFT
    assert CONV_WIDTH - 1 <= SCAN_PITCH_PAD and CONV_LEFT <= halo and conv_right <= halo
    for n in range(LRU_BLOCKS):
        cols = slice(n * LRU_BLOCK, (n + 1) * LRU_BLOCK)
        for j in range(SUBLANES):
            r0 = steps * j
            xs_ref[n, pitch * j + CONV_LEFT:pitch * j + CONV_LEFT + steps, :] = xm_ref[r0:r0 + steps, cols]
            if j == 0:
                left = jnp.where(at_start, 0.0, xp_ref[halo - CONV_LEFT:halo, cols])
            else:
                left = xm_ref[r0 - CONV_LEFT:r0, cols]
            xs_ref[n, pitch * j:pitch * j + CONV_LEFT, :] = left
            if j == SUBLANES - 1:
                right = jnp.where(at_end, 0.0, xn_ref[0:conv_right, cols])
            else:
                right = xm_ref[r0 + steps:r0 + steps + conv_right, cols]
            r1 = pitch * j + CONV_LEFT + steps
            xs_ref[n, r1:r1 + conv_right, :] = right

    cw = cw_ref[...]
    cb = cb_ref[...]
    sub = lax.broadcasted_iota(jnp.int32, (SUBLANES, LRU_BLOCK), 0)
    neg = -lam_ref[...]
    softplus = jnp.maximum(neg, 0.0) + jnp.log1p(jnp.exp(-jnp.abs(neg)))
    c2 = (-LRU_C * softplus) * (0.5 * LOG2E)
    order = range(steps - 1, -1, -1) if reverse else range(steps)
    for n in range(LRU_BLOCKS):
        cols = slice(n * LRU_BLOCK, (n + 1) * LRU_BLOCK)
        taps = [jnp.concatenate([xs_ref[n, pl.ds(s + tap, SUBLANES, stride=pitch), :] for s in range(steps)],
                                axis=0) * cw[tap:tap + 1, cols] for tap in range(CONV_WIDTH)]
        xq = functools.reduce(lambda x, y: x + y, taps) + cb[:, cols]
        gates = jnp.dot(xq.astype(BF16), w_ref[n], preferred_element_type=F32)
        tr = jnp.tanh(gates[:, :LRU_BLOCK] + ba_ref[:, cols])
        i = 0.5 * jnp.tanh(gates[:, LRU_BLOCK:] + bx_ref[:, cols]) + 0.5
        a = jnp.exp2(c2[:, cols] * tr + c2[:, cols])
        y = 1.0 - a * a
        u = (y * lax.rsqrt(jnp.maximum(y, F32_MIN_NORMAL))) * i * xq

        h = jnp.zeros((SUBLANES, LRU_BLOCK), F32)
        pr = jnp.ones((SUBLANES, LRU_BLOCK), F32)
        h_loc, p_loc = {}, {}
        for s in order:
            a_s = a[SUBLANES * s:SUBLANES * (s + 1)]
            h = a_s * h + u[SUBLANES * s:SUBLANES * (s + 1)]
            pr = a_s * pr
            h_loc[s], p_loc[s] = h, pr
        enter, leave = _sublane_scan(h, pr, carry_ref[:, cols], sub, reverse)
        carry_ref[:, cols] = leave
        for s in range(steps):
            hs_ref[n, pl.ds(s, SUBLANES, stride=pitch), :] = h_loc[s] + p_loc[s] * enter

    for j in range(SUBLANES):
        for n in range(LRU_BLOCKS):
            h_ref[steps * j:steps * (j + 1), n * LRU_BLOCK:(n + 1) * LRU_BLOCK] = (
                hs_ref[n, pitch * j:pitch * j + steps, :])


def _lru_direction(xl, cw, cb, w_gate, ba, bx, lam, layer, direction, geom, tc=512, halo=8):
    t = xl.shape[0]
    nc = t // tc
    rows_p, len_p, len_s = geom
    assert t % tc == 0 and len_p % tc == 0 and len_s % tc == 0 and tc % halo == 0
    reverse = direction == 1
    hb = tc // halo
    n_halo = t // halo
    stage_rows = SUBLANES * (tc // SUBLANES + SCAN_PITCH_PAD)

    def blk(c):
        return (nc - 1 - c) if reverse else c

    vec = pl.BlockSpec((None, None, 1, LRU_WIDTH), lambda c: (layer, direction, 0, 0))
    return pl.pallas_call(
        functools.partial(_lru_kernel, tc=tc, nc=nc, reverse=reverse, geom=geom),
        out_shape=jax.ShapeDtypeStruct((t, LRU_WIDTH), F32),
        grid=(nc,),
        in_specs=[
            pl.BlockSpec((tc, LRU_WIDTH), lambda c: (blk(c), 0)),
            pl.BlockSpec((halo, LRU_WIDTH), lambda c: (jnp.maximum(blk(c) * hb - 1, 0), 0)),
            pl.BlockSpec((halo, LRU_WIDTH), lambda c: (jnp.minimum((blk(c) + 1) * hb, n_halo - 1), 0)),
            pl.BlockSpec((None, CONV_WIDTH, LRU_WIDTH), lambda c: (layer, 0, 0)),
            pl.BlockSpec((None, 1, LRU_WIDTH), lambda c: (layer, 0, 0)),
            pl.BlockSpec((None, None, LRU_BLOCKS, LRU_BLOCK, 2 * LRU_BLOCK),
                         lambda c: (layer, direction, 0, 0, 0)),
            vec, vec, vec,
        ],
        out_specs=pl.BlockSpec((tc, LRU_WIDTH), lambda c: (blk(c), 0)),
        scratch_shapes=[pltpu.VMEM((1, LRU_WIDTH), F32),
                        pltpu.VMEM((LRU_BLOCKS, stage_rows, LRU_BLOCK), F32),
                        pltpu.VMEM((LRU_BLOCKS, stage_rows, LRU_BLOCK), F32)],
        compiler_params=_params("arbitrary"),
        name="lru_bwd" if reverse else "lru_fwd",
    )(xl, xl, xl, cw, cb, w_gate, ba, bx, lam)


def _merge_kernel(*refs, alpha):
    n_pat = len(DILATIONS)
    x_ref = refs[0]
    o_refs = refs[1:1 + n_pat]
    l_refs = refs[1 + n_pat:1 + 2 * n_pat]
    (hf_ref, hb_ref, gl_ref, ag_ref, lg_ref, wo_ref, g_ref, b_ref, out_ref,
     m0_ref, m1_ref, os_ref, ls_ref) = refs[1 + 2 * n_pat:]
    tm = x_ref.shape[0]
    i = pl.program_id(0)

    def normalise(m_ref):
        for g, dil in enumerate(DILATIONS):
            if dil == 1:
                continue
            for r in range(dil):
                ls_ref[g, pl.ds(r, tm // dil, stride=dil), :] = l_refs[g][:, r * LANES:(r + 1) * LANES]
                for p in range(HEAD_PAIRS):
                    c0 = r * ATT_WIDTH + p * LANES
                    os_ref[g, p, pl.ds(r, tm // dil, stride=dil), :] = o_refs[g][:, c0:c0 + LANES]

        lses = [l_refs[g][...] if dil == 1 else ls_ref[g] for g, dil in enumerate(DILATIONS)]
        mx = functools.reduce(jnp.maximum, lses)
        es = [jnp.exp(l - mx) for l in lses]
        den = functools.reduce(lambda a, b: a + b, es)
        ws = [e / den for e in es]
        low_half = lax.broadcasted_iota(jnp.int32, (tm, LANES), 1) < HEAD_DIM
        pieces = []
        for p in range(HEAD_PAIRS):
            acc = None
            for g, (dil, w) in enumerate(zip(DILATIONS, ws)):
                o_gp = o_refs[g][:, p * LANES:(p + 1) * LANES] if dil == 1 else os_ref[g, p]
                wsel = jnp.where(low_half, w[:, 2 * p:2 * p + 1], w[:, 2 * p + 1:2 * p + 2])
                term = o_gp * wsel
                acc = term if acc is None else acc + term
            pieces.append(acc)
        attn = jnp.concatenate(pieces, axis=1)
        m_ref[:, :ATT_WIDTH] = _rms_norm(attn, ag_ref[...]).astype(BF16)
        rec = (hf_ref[...] + hb_ref[...]) * jax.nn.gelu(gl_ref[...])
        m_ref[:, ATT_WIDTH:] = _rms_norm(rec, lg_ref[...]).astype(BF16)

    def project(m_ref):
        mix = jnp.dot(m_ref[...], wo_ref[...], preferred_element_type=F32)
        out_ref[...] = _layer_norm(alpha * x_ref[...] + mix, g_ref[...], b_ref[...])

    @pl.when(i == 0)
    def _():
        m1_ref[...] = jnp.zeros_like(m1_ref)

    parity = lax.rem(i, 2)

    @pl.when(parity == 0)
    def _():
        normalise(m0_ref)
        project(m1_ref)

    @pl.when(parity == 1)
    def _():
        normalise(m1_ref)
        project(m0_ref)


def _merge(x, os_, lses, hf, hb, gl, ag, lg, wo, g, b, layer, alpha, tm=256):
    t, d = x.shape
    assert t % tm == 0
    nt = t // tm
    n_pat = len(DILATIONS)

    def fresh(width, dil=1):
        return pl.BlockSpec((tm // dil, dil * width), lambda i: (jnp.minimum(i, nt - 1), 0))

    lagged = pl.BlockSpec((tm, d), lambda i: (jnp.maximum(i - 1, 0), 0))

    def vec(width):
        return pl.BlockSpec((None, 1, width), lambda i: (layer, 0, 0))

    return pl.pallas_call(
        functools.partial(_merge_kernel, alpha=alpha),
        out_shape=jax.ShapeDtypeStruct((t, d), F32),
        grid=(nt + 1,),
        in_specs=[lagged] + [fresh(ATT_WIDTH, dil) for dil in DILATIONS]
        + [fresh(LANES, dil) for dil in DILATIONS] + [fresh(LRU_WIDTH)] * 3
        + [vec(ATT_WIDTH), vec(LRU_WIDTH),
           pl.BlockSpec((None, d, d), lambda i: (layer, 0, 0), pipeline_mode=pl.Buffered(1)),
           vec(d), vec(d)],
        out_specs=lagged,
        scratch_shapes=[pltpu.VMEM((tm, d), BF16), pltpu.VMEM((tm, d), BF16),
                        pltpu.VMEM((n_pat, HEAD_PAIRS, tm, LANES), F32),
                        pltpu.VMEM((n_pat, tm, LANES), F32)],
        compiler_params=_params("arbitrary"),
        name="merge",
    )(x, *os_, *lses, hf, hb, gl, ag, lg, wo, g, b)


def kernel(x_prompt, x_sample, ln1_g, ln1_b, ffn1_w_gate, ffn1_w_up, ffn1_w_down, ln2_g, ln2_b, w_in, conv_w, conv_b, lru_w_a, lru_b_a, lru_w_x, lru_b_x, lru_lambda, att_norm_g, lru_norm_g, w_out, ln3_g, ln3_b, ffn2_w_gate, ffn2_w_up, ffn2_w_down):
    nb, s, d = x_prompt.shape
    db, ds, _ = x_sample.shape
    depth = ln1_g.shape[0]
    alpha = (2 * depth) ** 0.25
    assert d == ATT_WIDTH + LRU_WIDTH
    geom = (nb * s, s, ds)

    tables = _rope_tables(max(s, ds))
    row3 = lambda a: a.reshape(depth, 1, a.shape[-1])
    ln1 = (row3(ln1_g), row3(ln1_b))
    ln2 = (row3(ln2_g), row3(ln2_b))
    ln3 = (row3(ln3_g), row3(ln3_b))
    ffn1 = (_ffn_column_tiles(ffn1_w_gate), _ffn_column_tiles(ffn1_w_up), ffn1_w_down.astype(BF16))
    ffn2 = (_ffn_column_tiles(ffn2_w_gate), _ffn_column_tiles(ffn2_w_up), ffn2_w_down.astype(BF16))
    w_in_b = w_in.astype(BF16)
    w_qkv, w_lru = w_in_b[:, :, :3 * ATT_WIDTH], w_in_b[:, :, 3 * ATT_WIDTH:]
    w_gate = (0.5 * jnp.concatenate([lru_w_a, lru_w_x], axis=-1)).astype(BF16)
    vec4 = lambda a: a.reshape(depth, 2, 1, LRU_WIDTH)
    ba, bx, lam = vec4(0.5 * lru_b_a), vec4(0.5 * lru_b_x), vec4(lru_lambda)
    cb = row3(conv_b)
    ag, lg = row3(att_norm_g), row3(lru_norm_g)
    w_out_b = w_out.astype(BF16)

    xs = (x_prompt.reshape(nb * s, d), x_sample.reshape(db * ds, d))
    for layer in range(depth):
        (x,) = _ffn(xs, *ffn1, *ln1, layer, alpha, nb * s)
        qs, ks, vs, xl, gl = _inproj(x, w_qkv, w_lru, tables, layer, geom)
        os_, lses = [], []
        for g, dil in enumerate(DILATIONS):
            o, lse = _attn_pattern(qs[g], ks[g], vs[g], dil, geom)
            os_.append(o)
            lses.append(lse)
        hf = _lru_direction(xl, conv_w, cb, w_gate, ba, bx, lam, layer, 0, geom)
        hb = _lru_direction(xl, conv_w, cb, w_gate, ba, bx, lam, layer, 1, geom)
        x = _merge(x, os_, lses, hf, hb, gl, ag, lg, w_out_b, *ln2, layer, alpha)
        xs = _ffn((x,), *ffn2, *ln3, layer, alpha, nb * s, split_out=layer == depth - 1)

    y_prompt, y_sample = xs
    return (y_prompt.reshape(nb, s, d), y_sample.reshape(db, ds, d))
```

```python
import functools

import jax
import jax.numpy as jnp
from jax import lax
from jax.experimental import pallas as pl
from jax.experimental.pallas import tpu as pltpu

HEAD_DIM = 64
ATT_HEADS = 12
ATT_WIDTH = ATT_HEADS * HEAD_DIM
LRU_BLOCKS = 10
LRU_BLOCK = 128
LRU_WIDTH = LRU_BLOCKS * LRU_BLOCK
CONV_WIDTH = 4
CONV_LEFT = 2
LRU_C = 8.0
ROPE_THETA = 500000.0
ROT_DIM = HEAD_DIM // 4
DILATED_PATTERNS = ((128, 1), (512, 4), (2048, 16))
DILATIONS = tuple(d for _, d in DILATED_PATTERNS)
RADIUS = 64
assert all(w // (2 * d) == RADIUS for w, d in DILATED_PATTERNS)
NORM_EPS = 1e-5
NEG_INF = -1e30
F32_MIN_NORMAL = 1.1754943508222875e-38
LN2 = 0.6931471805599453
LOG2E = 1.4426950408889634

LANES = 128
SUBLANES = 8
HEAD_PAIRS = ATT_WIDTH // LANES
VMEM_LIMIT_BYTES = 56 * 1024 * 1024
FFN_TF = 512
CAST_BLOCK_BYTES = 6 * 1024 * 1024
SCAN_PITCH_PAD = 4

F32 = jnp.float32
BF16 = jnp.bfloat16


def _params(*semantics):
    return pltpu.CompilerParams(dimension_semantics=semantics, vmem_limit_bytes=VMEM_LIMIT_BYTES)


def _layer_norm(y, g, b):
    mu = jnp.mean(y, axis=-1, keepdims=True)
    yc = y - mu
    var = jnp.mean(yc * yc, axis=-1, keepdims=True)
    return yc * lax.rsqrt(var + NORM_EPS) * g + b


def _rms_norm(y, g):
    ms = jnp.mean(y * y, axis=-1, keepdims=True)
    return y * lax.rsqrt(ms + NORM_EPS) * g


def _seq_bounds(row0, rows_p, len_p, len_s):
    in_p = row0 < rows_p
    lo_p = lax.div(row0, len_p) * len_p
    lo_s = rows_p + lax.div(jnp.maximum(row0 - rows_p, 0), len_s) * len_s
    lo = jnp.where(in_p, lo_p, lo_s)
    hi = jnp.where(in_p, lo_p + len_p, lo_s + len_s)
    return lo, hi


def _cast_kernel(w_ref, o_ref):
    o_ref[...] = w_ref[...].astype(o_ref.dtype)


def _to_bf16(w):
    depth, rows, cols = w.shape
    tr = rows
    while tr * cols * 4 > CAST_BLOCK_BYTES and tr % 2 == 0:
        tr //= 2
    assert tr % 16 == 0 and rows % tr == 0
    spec = pl.BlockSpec((None, tr, cols), lambda l, i: (l, i, 0))
    return pl.pallas_call(
        _cast_kernel,
        out_shape=jax.ShapeDtypeStruct(w.shape, BF16),
        grid=(depth, rows // tr),
        in_specs=[spec],
        out_specs=spec,
        compiler_params=_params("parallel", "parallel"),
        name="to_bf16",
    )(w)


def _ffn_kernel(*refs, alpha, nj, n_p, split_in, split_out):
    n_x = 2 if split_in else 1
    n_o = 2 if split_out else 1
    x_refs = refs[:n_x]
    wg_ref, wu_ref, wd_ref, g_ref, b_ref = refs[n_x:n_x + 5]
    o_refs = refs[n_x + 5:n_x + 5 + n_o]
    xb_ref, acc_ref = refs[n_x + 5 + n_o:]
    i, j = pl.program_id(0), pl.program_id(1)
    groups = (i < n_p, i >= n_p)

    def start(x_ref):
        xb_ref[...] = x_ref[...].astype(BF16)
        acc_ref[...] = jnp.zeros_like(acc_ref)

    def finish(x_ref, o_ref):
        y = alpha * x_ref[...] + 0.5 * acc_ref[...]
        o_ref[...] = _layer_norm(y, g_ref[...], b_ref[...])

    if split_in:
        for x_ref, grp in zip(x_refs, groups):
            pl.when((j == 0) & grp)(functools.partial(start, x_ref))
    else:
        pl.when(j == 0)(functools.partial(start, x_refs[0]))

    xb = xb_ref[...]
    gate = jnp.dot(xb, wg_ref[...], preferred_element_type=F32)
    up = jnp.dot(xb, wu_ref[...], preferred_element_type=F32)
    h = (gate * jax.nn.sigmoid(gate)) * up
    acc_ref[...] += jnp.dot(h.astype(BF16), wd_ref[...], preferred_element_type=F32)

    last = j == nj - 1
    if split_in or split_out:
        for g_idx, grp in enumerate(groups):
            x_ref = x_refs[g_idx] if split_in else x_refs[0]
            o_ref = o_refs[g_idx] if split_out else o_refs[0]
            pl.when(last & grp)(functools.partial(finish, x_ref, o_ref))
    else:
        pl.when(last)(functools.partial(finish, x_refs[0], o_refs[0]))


def _ffn(xs, wg, wu, wd, g, b, layer, alpha, n_p_rows, split_out=False, tm=512, tf=FFN_TF):
    split_in = len(xs) == 2
    d = xs[0].shape[1]
    t = sum(x.shape[0] for x in xs)
    f = wg.shape[-1]
    nj = f // tf
    assert t % tm == 0 and f % tf == 0 and n_p_rows % tm == 0
    n_p = n_p_rows // tm

    def grouped_specs():
        return [pl.BlockSpec((tm, d), lambda i, j: (jnp.minimum(i, n_p - 1), 0)),
                pl.BlockSpec((tm, d), lambda i, j: (jnp.maximum(i - n_p, 0), 0))]

    flat_spec = [pl.BlockSpec((tm, d), lambda i, j: (i, 0))]
    if split_out:
        out_shape = [jax.ShapeDtypeStruct((n_p_rows, d), F32), jax.ShapeDtypeStruct((t - n_p_rows, d), F32)]
    else:
        out_shape = [jax.ShapeDtypeStruct((t, d), F32)]
    return pl.pallas_call(
        functools.partial(_ffn_kernel, alpha=alpha, nj=nj, n_p=n_p, split_in=split_in, split_out=split_out),
        out_shape=out_shape,
        grid=(t // tm, nj),
        in_specs=(grouped_specs() if split_in else flat_spec) + [
            pl.BlockSpec((None, d, tf), lambda i, j: (layer, 0, j)),
            pl.BlockSpec((None, d, tf), lambda i, j: (layer, 0, j)),
            pl.BlockSpec((None, tf, d), lambda i, j: (layer, j, 0)),
            pl.BlockSpec((None, 1, d), lambda i, j: (layer, 0, 0)),
            pl.BlockSpec((None, 1, d), lambda i, j: (layer, 0, 0)),
        ],
        out_specs=grouped_specs() if split_out else flat_spec,
        scratch_shapes=[pltpu.VMEM((tm, d), BF16), pltpu.VMEM((tm, d), F32)],
        compiler_params=_params("arbitrary", "arbitrary"),
        name="ffn",
    )(*xs, wg, wu, wd, g, b)


def _rotary(t, cos_t, sin_lo, sin_hi):
    half = ROT_DIM // 2
    out = []
    for p in range(HEAD_PAIRS):
        tp = t[:, p * LANES:(p + 1) * LANES]
        up = pltpu.roll(tp, LANES - half, 1)
        dn = pltpu.roll(tp, half, 1)
        out.append(tp * cos_t + up * sin_lo + dn * sin_hi)
    return jnp.concatenate(out, axis=1)


def _emit_dilated(t, out_refs, stage_ref):
    rows = t.shape[0]
    for dil, out_ref in zip(DILATIONS, out_refs):
        if dil == 1:
            out_ref[...] = t.astype(BF16)
    for p in range(HEAD_PAIRS):
        stage_ref[p] = t[:, p * LANES:(p + 1) * LANES]
    for dil, out_ref in zip(DILATIONS, out_refs):
        if dil == 1:
            continue
        for r in range(dil):
            for p in range(HEAD_PAIRS):
                piece = stage_ref[p, pl.ds(r, rows // dil, stride=dil), :]
                c0 = r * ATT_WIDTH + p * LANES
                out_ref[:, c0:c0 + LANES] = piece.astype(BF16)


def _inproj_kernel(x_ref, wq_ref, wk_ref, wv_ref, wx_ref, wg_ref, cos_ref, slo_ref, shi_ref, *rest):
    n_pat = len(DILATIONS)
    q_refs, k_refs, v_refs = rest[:n_pat], rest[n_pat:2 * n_pat], rest[2 * n_pat:3 * n_pat]
    xl_ref, gl_ref, stage_ref = rest[3 * n_pat:]
    xb = x_ref[...].astype(BF16)
    cos_t, sin_lo, sin_hi = cos_ref[...], slo_ref[...], shi_ref[...]
    q = jnp.dot(xb, wq_ref[...], preferred_element_type=F32)
    _emit_dilated(_rotary(q, cos_t, sin_lo, sin_hi) * (HEAD_DIM ** -0.5 * LOG2E), q_refs, stage_ref)
    k = jnp.dot(xb, wk_ref[...], preferred_element_type=F32)
    _emit_dilated(_rotary(k, cos_t, sin_lo, sin_hi), k_refs, stage_ref)
    _emit_dilated(jnp.dot(xb, wv_ref[...], preferred_element_type=F32), v_refs, stage_ref)
    xl_ref[...] = jnp.dot(xb, wx_ref[...], preferred_element_type=F32)
    gl_ref[...] = jnp.dot(xb, wg_ref[...], preferred_element_type=F32)


def _rope_tables(n_pos):
    half = ROT_DIM // 2
    inv_freq = ROPE_THETA ** (-jnp.arange(0, ROT_DIM, 2, dtype=F32) / ROT_DIM)
    ang = jnp.arange(n_pos, dtype=F32)[:, None] * inv_freq[None, :]
    cos, sin = jnp.cos(ang), jnp.sin(ang)
    rest = HEAD_DIM - ROT_DIM
    one = jnp.ones((n_pos, rest), F32)
    zero = jnp.zeros((n_pos, rest), F32)
    zh = jnp.zeros((n_pos, half), F32)
    reps = LANES // HEAD_DIM
    cos_t = jnp.tile(jnp.concatenate([cos, cos, one], axis=1), (1, reps))
    sin_lo = jnp.tile(jnp.concatenate([-sin, zh, zero], axis=1), (1, reps))
    sin_hi = jnp.tile(jnp.concatenate([zh, sin, zero], axis=1), (1, reps))
    return cos_t, sin_lo, sin_hi


def _inproj(x, w_qkv, w_lru, tables, layer, geom, tm=256):
    t, d = x.shape
    rows_p, len_p, len_s = geom
    assert t % tm == 0 and len_p % tm == 0 and len_s % tm == 0
    n_p, bp, bs = rows_p // tm, len_p // tm, len_s // tm

    def pos_map(i):
        return (jnp.where(i < n_p, lax.rem(i, bp), lax.rem(jnp.maximum(i - n_p, 0), bs)), 0)

    def wspec(width, col):
        return pl.BlockSpec((None, d, width), lambda i: (layer, 0, col), pipeline_mode=pl.Buffered(1))

    tab_spec = pl.BlockSpec((tm, LANES), pos_map)
    att_shapes = [jax.ShapeDtypeStruct((t // dil, dil * ATT_WIDTH), BF16) for dil in DILATIONS]
    att_specs = [pl.BlockSpec((tm // dil, dil * ATT_WIDTH), lambda i: (i, 0)) for dil in DILATIONS]
    lru_spec = pl.BlockSpec((tm, LRU_WIDTH), lambda i: (i, 0))
    outs = pl.pallas_call(
        _inproj_kernel,
        out_shape=att_shapes * 3 + [jax.ShapeDtypeStruct((t, LRU_WIDTH), F32)] * 2,
        grid=(t // tm,),
        in_specs=[
            pl.BlockSpec((tm, d), lambda i: (i, 0)),
            wspec(ATT_WIDTH, 0), wspec(ATT_WIDTH, 1), wspec(ATT_WIDTH, 2),
            wspec(LRU_WIDTH, 0), wspec(LRU_WIDTH, 1),
            tab_spec, tab_spec, tab_spec,
        ],
        out_specs=att_specs * 3 + [lru_spec, lru_spec],
        scratch_shapes=[pltpu.VMEM((HEAD_PAIRS, tm, LANES), F32)],
        compiler_params=_params("parallel"),
        name="inproj",
    )(x, w_qkv, w_qkv, w_qkv, w_lru, w_lru, *tables)
    n_pat = len(DILATIONS)
    return outs[:n_pat], outs[n_pat:2 * n_pat], outs[2 * n_pat:3 * n_pat], outs[-2], outs[-1]


def _attn_kernel(q_ref, km_ref, kl_ref, kr_ref, vm_ref, vl_ref, vr_ref, o_ref, lse_ref,
                 kc_ref, vc_ref, *, tq, geom):
    rows_p, len_p, len_s = geom
    row0 = pl.program_id(0) * tq
    lo, hi = _seq_bounds(row0, rows_p, len_p, len_s)

    kc_ref[0:RADIUS] = kl_ref[...]
    kc_ref[RADIUS:RADIUS + tq] = km_ref[...]
    kc_ref[RADIUS + tq:] = kr_ref[...]
    vc_ref[0:RADIUS] = vl_ref[...]
    vc_ref[RADIUS:RADIUS + tq] = vm_ref[...]
    vc_ref[RADIUS + tq:] = vr_ref[...]

    qb_rows = LANES
    kb_rows = qb_rows + 2 * RADIUS
    lane = lax.broadcasted_iota(jnp.int32, (qb_rows, LANES), 1)
    low_half = lane < HEAD_DIM
    trow = lax.broadcasted_iota(jnp.int32, (qb_rows, kb_rows), 0)
    ccol = lax.broadcasted_iota(jnp.int32, (qb_rows, kb_rows), 1)
    band = (ccol >= trow) & (ccol <= trow + 2 * RADIUS)

    for qb in range(tq // qb_rows):
        r0 = qb * qb_rows
        base = row0 + r0 - RADIUS
        valid = band & (ccol >= lo - base) & (ccol < hi - base)
        valid2 = jnp.concatenate([valid, valid], axis=0)
        m_tile = jnp.zeros((qb_rows, LANES), F32)
        l_tile = jnp.ones((qb_rows, LANES), F32)
        for p in range(HEAD_PAIRS):
            cols = slice(p * LANES, (p + 1) * LANES)
            qp = q_ref[r0:r0 + qb_rows, cols]
            kp = kc_ref[r0:r0 + kb_rows, cols]
            vp = vc_ref[r0:r0 + kb_rows, cols]
            zero = jnp.zeros_like(qp)
            q2 = jnp.concatenate([jnp.where(low_half, qp, zero), jnp.where(low_half, zero, qp)], axis=0)
            s = lax.dot_general(q2, kp, (((1,), (1,)), ((), ())), preferred_element_type=F32)
            s = jnp.where(valid2, s, NEG_INF)
            m = jnp.max(s, axis=1, keepdims=True)
            e = jnp.exp2(s - m)
            l = jnp.sum(e, axis=1, keepdims=True)
            pv = jnp.dot(e.astype(BF16), vp, preferred_element_type=F32) / l
            o_ref[r0:r0 + qb_rows, cols] = jnp.where(low_half, pv[:qb_rows], pv[qb_rows:])
            for hh, rows in enumerate((slice(0, qb_rows), slice(qb_rows, 2 * qb_rows))):
                m_tile = jnp.where(lane == 2 * p + hh, m[rows], m_tile)
                l_tile = jnp.where(lane == 2 * p + hh, l[rows], l_tile)
        lse_ref[r0:r0 + qb_rows, :] = m_tile * LN2 + jnp.log(l_tile)


def _attn_pattern(q, k, v, dil, geom, max_tq=512):
    rows = q.shape[0]
    rows_p, len_p, len_s = (g // dil for g in geom)
    tq = min(max_tq, len_p, len_s)
    assert rows % tq == 0 and len_p % tq == 0 and len_s % tq == 0 and tq % RADIUS == 0
    hb = tq // RADIUS
    n_halo = rows // RADIUS
    main = pl.BlockSpec((tq, ATT_WIDTH), lambda i, r: (i, r))
    left = pl.BlockSpec((RADIUS, ATT_WIDTH), lambda i, r: (jnp.maximum(i * hb - 1, 0), r))
    right = pl.BlockSpec((RADIUS, ATT_WIDTH), lambda i, r: (jnp.minimum((i + 1) * hb, n_halo - 1), r))
    return pl.pallas_call(
        functools.partial(_attn_kernel, tq=tq, geom=(rows_p, len_p, len_s)),
        out_shape=[jax.ShapeDtypeStruct((rows, dil * ATT_WIDTH), F32),
                   jax.ShapeDtypeStruct((rows, dil * LANES), F32)],
        grid=(rows // tq, dil),
        in_specs=[main, main, left, right, main, left, right],
        out_specs=[pl.BlockSpec((tq, ATT_WIDTH), lambda i, r: (i, r)),
                   pl.BlockSpec((tq, LANES), lambda i, r: (i, r))],
        scratch_shapes=[pltpu.VMEM((tq + 2 * RADIUS, ATT_WIDTH), BF16),
                        pltpu.VMEM((tq + 2 * RADIUS, ATT_WIDTH), BF16)],
        compiler_params=_params("parallel", "parallel"),
        name=f"attn_d{dil}",
    )(q, k, k, k, v, v, v)


def _sublane_scan(h_end, p_end, carry, sub, reverse):
    n = SUBLANES
    k = 1
    while k < n:
        if reverse:
            keep, shift = sub < n - k, n - k
        else:
            keep, shift = sub >= k, k
        h_sh = jnp.where(keep, pltpu.roll(h_end, shift, 0), 0.0)
        p_sh = jnp.where(keep, pltpu.roll(p_end, shift, 0), 1.0)
        h_end = p_end * h_sh + h_end
        p_end = p_end * p_sh
        k *= 2
    leave = h_end + p_end * carry
    if reverse:
        enter = jnp.where(sub < n - 1, pltpu.roll(leave, n - 1, 0), carry)
        return enter, leave[0:1]
    enter = jnp.where(sub >= 1, pltpu.roll(leave, 1, 0), carry)
    return enter, leave[n - 1:n]


def _lru_kernel(xm_ref, xp_ref, xn_ref, cw_ref, cb_ref, w_ref, ba_ref, bx_ref, lam_ref, h_ref,
                carry_ref, xs_ref, hs_ref, *, tc, nc, reverse, geom):
    rows_p, len_p, len_s = geom
    c = pl.program_id(0)
    row0 = ((nc - 1 - c) if reverse else c) * tc
    lo, hi = _seq_bounds(row0, rows_p, len_p, len_s)
    at_start = row0 == lo
    at_end = row0 + tc == hi

    @pl.when(at_end if reverse else at_start)
    def _():
        carry_ref[...] = jnp.zeros_like(carry_ref)

    halo = xp_ref.shape[0]
    steps = tc // SUBLANES
    pitch = steps + SCAN_PITCH_PAD
    conv_right = CONV_WIDTH - 1 - CONV_LEFT
    assert CONV_WIDTH - 1 <= SCAN_PITCH_PAD and CONV_LEFT <= halo and conv_right <= halo
    for n in range(LRU_BLOCKS):
        cols = slice(n * LRU_BLOCK, (n + 1) * LRU_BLOCK)
        for j in range(SUBLANES):
            r0 = steps * j
            xs_ref[n, pitch * j + CONV_LEFT:pitch * j + CONV_LEFT + steps, :] = xm_ref[r0:r0 + steps, cols]
            if j == 0:
                left = jnp.where(at_start, 0.0, xp_ref[halo - CONV_LEFT:halo, cols])
            else:
                left = xm_ref[r0 - CONV_LEFT:r0, cols]
            xs_ref[n, pitch * j:pitch * j + CONV_LEFT, :] = left
            if j == SUBLANES - 1:
                right = jnp.where(at_end, 0.0, xn_ref[0:conv_right, cols])
            else:
                right = xm_ref[r0 + steps:r0 + steps + conv_right, cols]
            r1 = pitch * j + CONV_LEFT + steps
            xs_ref[n, r1:r1 + conv_right, :] = right

    cw = cw_ref[...]
    cb = cb_ref[...]
    sub = lax.broadcasted_iota(jnp.int32, (SUBLANES, LRU_BLOCK), 0)
    neg = -lam_ref[...]
    softplus = jnp.maximum(neg, 0.0) + jnp.log1p(jnp.exp(-jnp.abs(neg)))
    c2 = (-LRU_C * softplus) * (0.5 * LOG2E)
    order = range(steps - 1, -1, -1) if reverse else range(steps)
    for n in range(LRU_BLOCKS):
        cols = slice(n * LRU_BLOCK, (n + 1) * LRU_BLOCK)
        taps = [jnp.concatenate([xs_ref[n, pl.ds(s + tap, SUBLANES, stride=pitch), :] for s in range(steps)],
                                axis=0) * cw[tap:tap + 1, cols] for tap in range(CONV_WIDTH)]
        xq = functools.reduce(lambda x, y: x + y, taps) + cb[:, cols]
        gates = jnp.dot(xq.astype(BF16), w_ref[n], preferred_element_type=F32)
        tr = jnp.tanh(gates[:, :LRU_BLOCK] + ba_ref[:, cols])
        i = 0.5 * jnp.tanh(gates[:, LRU_BLOCK:] + bx_ref[:, cols]) + 0.5
        a = jnp.exp2(c2[:, cols] * tr + c2[:, cols])
        y = 1.0 - a * a
        u = (y * lax.rsqrt(jnp.maximum(y, F32_MIN_NORMAL))) * i * xq

        h = jnp.zeros((SUBLANES, LRU_BLOCK), F32)
        pr = jnp.ones((SUBLANES, LRU_BLOCK), F32)
        h_loc, p_loc = {}, {}
        for s in order:
            a_s = a[SUBLANES * s:SUBLANES * (s + 1)]
            h = a_s * h + u[SUBLANES * s:SUBLANES * (s + 1)]
            pr = a_s * pr
            h_loc[s], p_loc[s] = h, pr
        enter, leave = _sublane_scan(h, pr, carry_ref[:, cols], sub, reverse)
        carry_ref[:, cols] = leave
        for s in range(steps):
            hs_ref[n, pl.ds(s, SUBLANES, stride=pitch), :] = h_loc[s] + p_loc[s] * enter

    for j in range(SUBLANES):
        for n in range(LRU_BLOCKS):
            h_ref[steps * j:steps * (j + 1), n * LRU_BLOCK:(n + 1) * LRU_BLOCK] = (
                hs_ref[n, pitch * j:pitch * j + steps, :])


def _lru_direction(xl, cw, cb, w_gate, ba, bx, lam, layer, direction, geom, tc=512, halo=8):
    t = xl.shape[0]
    nc = t // tc
    rows_p, len_p, len_s = geom
    assert t % tc == 0 and len_p % tc == 0 and len_s % tc == 0 and tc % halo == 0
    reverse = direction == 1
    hb = tc // halo
    n_halo = t // halo
    stage_rows = SUBLANES * (tc // SUBLANES + SCAN_PITCH_PAD)

    def blk(c):
        return (nc - 1 - c) if reverse else c

    vec = pl.BlockSpec((None, None, 1, LRU_WIDTH), lambda c: (layer, direction, 0, 0))
    return pl.pallas_call(
        functools.partial(_lru_kernel, tc=tc, nc=nc, reverse=reverse, geom=geom),
        out_shape=jax.ShapeDtypeStruct((t, LRU_WIDTH), F32),
        grid=(nc,),
        in_specs=[
            pl.BlockSpec((tc, LRU_WIDTH), lambda c: (blk(c), 0)),
            pl.BlockSpec((halo, LRU_WIDTH), lambda c: (jnp.maximum(blk(c) * hb - 1, 0), 0)),
            pl.BlockSpec((halo, LRU_WIDTH), lambda c: (jnp.minimum((blk(c) + 1) * hb, n_halo - 1), 0)),
            pl.BlockSpec((None, CONV_WIDTH, LRU_WIDTH), lambda c: (layer, 0, 0)),
            pl.BlockSpec((None, 1, LRU_WIDTH), lambda c: (layer, 0, 0)),
            pl.BlockSpec((None, None, LRU_BLOCKS, LRU_BLOCK, 2 * LRU_BLOCK),
                         lambda c: (layer, direction, 0, 0, 0)),
            vec, vec, vec,
        ],
        out_specs=pl.BlockSpec((tc, LRU_WIDTH), lambda c: (blk(c), 0)),
        scratch_shapes=[pltpu.VMEM((1, LRU_WIDTH), F32),
                        pltpu.VMEM((LRU_BLOCKS, stage_rows, LRU_BLOCK), F32),
                        pltpu.VMEM((LRU_BLOCKS, stage_rows, LRU_BLOCK), F32)],
        compiler_params=_params("arbitrary"),
        name="lru_bwd" if reverse else "lru_fwd",
    )(xl, xl, xl, cw, cb, w_gate, ba, bx, lam)


def _merge_kernel(*refs, alpha):
    n_pat = len(DILATIONS)
    x_ref = refs[0]
    o_refs = refs[1:1 + n_pat]
    l_refs = refs[1 + n_pat:1 + 2 * n_pat]
    (hf_ref, hb_ref, gl_ref, ag_ref, lg_ref, wo_ref, g_ref, b_ref, out_ref,
     m0_ref, m1_ref, os_ref, ls_ref) = refs[1 + 2 * n_pat:]
    tm = x_ref.shape[0]
    i = pl.program_id(0)

    def normalise(m_ref):
        for g, dil in enumerate(DILATIONS):
            if dil == 1:
                continue
            for r in range(dil):
                ls_ref[g, pl.ds(r, tm // dil, stride=dil), :] = l_refs[g][:, r * LANES:(r + 1) * LANES]
                for p in range(HEAD_PAIRS):
                    c0 = r * ATT_WIDTH + p * LANES
                    os_ref[g, p, pl.ds(r, tm // dil, stride=dil), :] = o_refs[g][:, c0:c0 + LANES]

        lses = [l_refs[g][...] if dil == 1 else ls_ref[g] for g, dil in enumerate(DILATIONS)]
        mx = functools.reduce(jnp.maximum, lses)
        es = [jnp.exp(l - mx) for l in lses]
        den = functools.reduce(lambda a, b: a + b, es)
        ws = [e / den for e in es]
        low_half = lax.broadcasted_iota(jnp.int32, (tm, LANES), 1) < HEAD_DIM
        pieces = []
        for p in range(HEAD_PAIRS):
            acc = None
            for g, (dil, w) in enumerate(zip(DILATIONS, ws)):
                o_gp = o_refs[g][:, p * LANES:(p + 1) * LANES] if dil == 1 else os_ref[g, p]
                wsel = jnp.where(low_half, w[:, 2 * p:2 * p + 1], w[:, 2 * p + 1:2 * p + 2])
                term = o_gp * wsel
                acc = term if acc is None else acc + term
            pieces.append(acc)
        attn = jnp.concatenate(pieces, axis=1)
        m_ref[:, :ATT_WIDTH] = _rms_norm(attn, ag_ref[...]).astype(BF16)
        rec = (hf_ref[...] + hb_ref[...]) * jax.nn.gelu(gl_ref[...])
        m_ref[:, ATT_WIDTH:] = _rms_norm(rec, lg_ref[...]).astype(BF16)

    def project(m_ref):
        mix = jnp.dot(m_ref[...], wo_ref[...], preferred_element_type=F32)
        out_ref[...] = _layer_norm(alpha * x_ref[...] + mix, g_ref[...], b_ref[...])

    @pl.when(i == 0)
    def _():
        m1_ref[...] = jnp.zeros_like(m1_ref)

    parity = lax.rem(i, 2)

    @pl.when(parity == 0)
    def _():
        normalise(m0_ref)
        project(m1_ref)

    @pl.when(parity == 1)
    def _():
        normalise(m1_ref)
        project(m0_ref)


def _merge(x, os_, lses, hf, hb, gl, ag, lg, wo, g, b, layer, alpha, tm=256):
    t, d = x.shape
    assert t % tm == 0
    nt = t // tm
    n_pat = len(DILATIONS)

    def fresh(width, dil=1):
        return pl.BlockSpec((tm // dil, dil * width), lambda i: (jnp.minimum(i, nt - 1), 0))

    lagged = pl.BlockSpec((tm, d), lambda i: (jnp.maximum(i - 1, 0), 0))

    def vec(width):
        return pl.BlockSpec((None, 1, width), lambda i: (layer, 0, 0))

    return pl.pallas_call(
        functools.partial(_merge_kernel, alpha=alpha),
        out_shape=jax.ShapeDtypeStruct((t, d), F32),
        grid=(nt + 1,),
        in_specs=[lagged] + [fresh(ATT_WIDTH, dil) for dil in DILATIONS]
        + [fresh(LANES, dil) for dil in DILATIONS] + [fresh(LRU_WIDTH)] * 3
        + [vec(ATT_WIDTH), vec(LRU_WIDTH),
           pl.BlockSpec((None, d, d), lambda i: (layer, 0, 0), pipeline_mode=pl.Buffered(1)),
           vec(d), vec(d)],
        out_specs=lagged,
        scratch_shapes=[pltpu.VMEM((tm, d), BF16), pltpu.VMEM((tm, d), BF16),
                        pltpu.VMEM((n_pat, HEAD_PAIRS, tm, LANES), F32),
                        pltpu.VMEM((n_pat, tm, LANES), F32)],
        compiler_params=_params("arbitrary"),
        name="merge",
    )(x, *os_, *lses, hf, hb, gl, ag, lg, wo, g, b)


def kernel(x_prompt, x_sample, ln1_g, ln1_b, ffn1_w_gate, ffn1_w_up, ffn1_w_down, ln2_g, ln2_b, w_in, conv_w, conv_b, lru_w_a, lru_b_a, lru_w_x, lru_b_x, lru_lambda, att_norm_g, lru_norm_g, w_out, ln3_g, ln3_b, ffn2_w_gate, ffn2_w_up, ffn2_w_down):
    nb, s, d = x_prompt.shape
    db, ds, _ = x_sample.shape
    depth = ln1_g.shape[0]
    alpha = (2 * depth) ** 0.25
    assert d == ATT_WIDTH + LRU_WIDTH
    geom = (nb * s, s, ds)

    tables = _rope_tables(max(s, ds))
    row3 = lambda a: a.reshape(depth, 1, a.shape[-1])
    ln1 = (row3(ln1_g), row3(ln1_b))
    ln2 = (row3(ln2_g), row3(ln2_b))
    ln3 = (row3(ln3_g), row3(ln3_b))
    ffn1 = tuple(_to_bf16(w) for w in (ffn1_w_gate, ffn1_w_up, ffn1_w_down))
    ffn2 = tuple(_to_bf16(w) for w in (ffn2_w_gate, ffn2_w_up, ffn2_w_down))
    w_in_b = _to_bf16(w_in)
    w_qkv, w_lru = w_in_b[:, :, :3 * ATT_WIDTH], w_in_b[:, :, 3 * ATT_WIDTH:]
    w_gate = (0.5 * jnp.concatenate([lru_w_a, lru_w_x], axis=-1)).astype(BF16)
    vec4 = lambda a: a.reshape(depth, 2, 1, LRU_WIDTH)
    ba, bx, lam = vec4(0.5 * lru_b_a), vec4(0.5 * lru_b_x), vec4(lru_lambda)
    cb = row3(conv_b)
    ag, lg = row3(att_norm_g), row3(lru_norm_g)
    w_out_b = _to_bf16(w_out)

    xs = (x_prompt.reshape(nb * s, d), x_sample.reshape(db * ds, d))
    for layer in range(depth):
        (x,) = _ffn(xs, *ffn1, *ln1, layer, alpha, nb * s)
        qs, ks, vs, xl, gl = _inproj(x, w_qkv, w_lru, tables, layer, geom)
        os_, lses = [], []
        for g, dil in enumerate(DILATIONS):
            o, lse = _attn_pattern(qs[g], ks[g], vs[g], dil, geom)
            os_.append(o)
            lses.append(lse)
        hf = _lru_direction(xl, conv_w, cb, w_gate, ba, bx, lam, layer, 0, geom)
        hb = _lru_direction(xl, conv_w, cb, w_gate, ba, bx, lam, layer, 1, geom)
        x = _merge(x, os_, lses, hf, hb, gl, ag, lg, w_out_b, *ln2, layer, alpha)
        xs = _ffn((x,), *ffn2, *ln3, layer, alpha, nb * s, split_out=layer == depth - 1)

    y_prompt, y_sample = xs
    return (y_prompt.reshape(nb, s, d), y_sample.reshape(db, ds, d))
```
